```python
import math
import jax
import jax.numpy as jnp
from jax import lax
import numpy as np

D_MODEL = 1024
BATCH = 32
SEQ = 256
DEPTH = 2
DEC_BATCH = 2
DEC_SEQ = 4096
PAST_LEN = 512

GRID_W = 64
MIX_W = 1024
GROUP_W = 256
DIFF_HEADS = 4
DIFF_QK = 32
DIFF_V = 64
DIFF_REP = 2
FNET_GROUPS = 4
FNET_CH = 64
MLA_HEADS = 4
MLA_Q_RANK = 192
MLA_KV_RANK = 128
MLA_NOPE = 64
MLA_ROPE = 32
MLA_V = 64
GQA_HEADS = 4
GQA_KV_HEADS = 2
GQA_DIM = 64
GQA_REP = GQA_HEADS // GQA_KV_HEADS
D_FF = 4 * D_MODEL
N_MOD = 6
Q_BLOCK = 128
ROPE_THETA = 10000.0
EPS = 1e-6
DIFF_SCALE = DIFF_QK ** -0.5
MLA_SCALE = (MLA_NOPE + MLA_ROPE) ** -0.5
GQA_SCALE = GQA_DIM ** -0.5
SPLITS = (DIFF_HEADS * 2 * DIFF_QK, DIFF_HEADS * 2 * DIFF_QK, DIFF_HEADS * DIFF_V,
          FNET_GROUPS * FNET_CH, MLA_Q_RANK, MLA_KV_RANK, MLA_ROPE,
          GQA_HEADS * GQA_DIM, GQA_KV_HEADS * GQA_DIM, GQA_KV_HEADS * GQA_DIM)
IN_W = 1888

kernel_name = "hybrid_diffusion_parallel_groups_step"


def rmsnorm(x, g):
    xf = x.astype(jnp.float32)
    y = xf * lax.rsqrt(jnp.mean(xf * xf, axis=-1, keepdims=True) + EPS)
    return (y * g.astype(jnp.float32)).astype(x.dtype)


def adaln(c, w, b):
    mods = jax.nn.silu(c) @ w + b
    return jnp.split(mods, N_MOD, axis=-1)


def modulate(x, g, shift, scale):
    return rmsnorm(x, g) * (1.0 + scale) + shift


def rope_2d(x, row, col):
    d = x.shape[-1]
    a = d // 2
    inv = ROPE_THETA ** (-jnp.arange(0, a, 2, dtype=jnp.float32) / a)

    def rot(xa, pos):
        ang = pos.astype(jnp.float32)[:, None] * inv[None, :]
        cos = jnp.cos(ang)[None, :, None, :]
        sin = jnp.sin(ang)[None, :, None, :]
        x1 = xa[..., : a // 2].astype(jnp.float32)
        x2 = xa[..., a // 2:].astype(jnp.float32)
        return jnp.concatenate([x1 * cos - x2 * sin, x1 * sin + x2 * cos], axis=-1)

    out = jnp.concatenate([rot(x[..., :a], row), rot(x[..., a:], col)], axis=-1)
    return out.astype(x.dtype)


def blocked_attention(parts, scale):
    b, s, h, _ = parts[0][0].shape
    dv = parts[0][2].shape[-1]
    nb = s // Q_BLOCK
    q_blocks = tuple(jnp.moveaxis(q.reshape(b, nb, Q_BLOCK, h, q.shape[-1]), 1, 0) for q, _, _ in parts)
    keys = [k for _, k, _ in parts]
    vals = [v for _, _, v in parts]
    sizes = [k.shape[1] for k in keys]

    def one_block(qb):
        logits = jnp.concatenate(
            [jnp.einsum("bqhd,bkhd->bhqk", q_i, k_i, preferred_element_type=jnp.float32)
             for q_i, k_i in zip(qb, keys)], axis=-1) * scale
        probs = jax.nn.softmax(logits, axis=-1)
        out = None
        start = 0
        for v_i, n_i in zip(vals, sizes):
            o_i = jnp.einsum("bhqk,bkhd->bqhd", probs[..., start:start + n_i].astype(v_i.dtype), v_i)
            out = o_i if out is None else out + o_i
            start += n_i
        return out

    o = lax.map(one_block, q_blocks)
    return jnp.moveaxis(o, 0, 1).reshape(b, s, h, dv)


def split_cols(z):
    offs = []
    acc = 0
    for n in SPLITS[:-1]:
        acc += n
        offs.append(acc)
    return jnp.split(z, offs, axis=-1)


def mixer_inputs(h, lp):
    b, s, _ = h.shape
    a_q, a_k, a_v, b_u, c_q, c_kv, c_kr, d_q, d_k, d_v = split_cols(h @ lp["w_in"])
    q_mla = (rmsnorm(c_q, lp["mla_q_norm_g"]) @ lp["mla_w_uq"]).reshape(b, s, MLA_HEADS, MLA_NOPE + MLA_ROPE)
    return {
        "diff_q": a_q.reshape(b, s, DIFF_HEADS * 2, DIFF_QK),
        "diff_k": a_k.reshape(b, s, DIFF_HEADS, 2 * DIFF_QK),
        "diff_v": a_v.reshape(b, s, DIFF_HEADS, DIFF_V),
        "fnet_u": b_u,
        "mla_q_nope": q_mla[..., :MLA_NOPE],
        "mla_q_rope": q_mla[..., MLA_NOPE:],
        "mla_ckv": rmsnorm(c_kv, lp["mla_kv_norm_g"]),
        "mla_krope": c_kr,
        "gqa_q": rmsnorm(d_q.reshape(b, s, GQA_HEADS, GQA_DIM), lp["gqa_q_norm_g"]),
        "gqa_k": rmsnorm(d_k.reshape(b, s, GQA_KV_HEADS, GQA_DIM), lp["gqa_k_norm_g"]),
        "gqa_v": d_v.reshape(b, s, GQA_KV_HEADS, GQA_DIM),
    }


def diff_maps(k):
    b, s = k.shape[:2]
    return k.reshape(b, s, DIFF_HEADS * 2, DIFF_QK)


def mla_expand(ckv, w_ukv):
    b, s = ckv.shape[:2]
    kv = (ckv @ w_ukv).reshape(b, s, MLA_HEADS, MLA_NOPE + MLA_V)
    return kv[..., :MLA_NOPE], kv[..., MLA_NOPE:]


def bcast_heads(kr):
    return jnp.broadcast_to(kr, kr.shape[:2] + (MLA_HEADS, kr.shape[-1]))


def fourier_mix(u):
    b, s, _ = u.shape
    uf = u.astype(jnp.float32).reshape(b, s, FNET_GROUPS, FNET_CH)
    y = jnp.fft.fft2(uf, axes=(1, 3), norm="ortho").real
    return y.reshape(b, s, GROUP_W).astype(u.dtype)


def diff_combine(o, lam, g, lam_init):
    b, s = o.shape[:2]
    o = o.reshape(b, s, DIFF_HEADS, 2, DIFF_V)
    y = o[..., 0, :] - lam.astype(o.dtype) * o[..., 1, :]
    y = rmsnorm(y, g) * (1.0 - lam_init)
    return y.reshape(b, s, GROUP_W)


def mixer_output(o_diff, u, o_mla, o_gqa, lp, lam, lam_init):
    b, s = u.shape[:2]
    y = jnp.concatenate([
        diff_combine(o_diff, lam, lp["diff_subln_g"], lam_init),
        fourier_mix(u),
        o_mla.reshape(b, s, MLA_HEADS * MLA_V),
        o_gqa.reshape(b, s, GQA_HEADS * GQA_DIM)], axis=-1)
    return y @ lp["w_out"]


def sqrelu_mlp(h, lp):
    return jnp.square(jax.nn.relu(h @ lp["mlp_w1"])) @ lp["mlp_w2"]


def context_layer(x, mod, lp, lam, lam_init):
    sh1, sc1, g1, sh2, sc2, g2 = mod
    t = mixer_inputs(modulate(x, lp["norm_mix_g"], sh1, sc1), lp)
    o_a = blocked_attention([(t["diff_q"], diff_maps(t["diff_k"]), jnp.repeat(t["diff_v"], DIFF_REP, axis=2))], DIFF_SCALE)
    k_nope, v_mla = mla_expand(t["mla_ckv"], lp["mla_w_ukv"])
    q_mla = jnp.concatenate([t["mla_q_nope"], t["mla_q_rope"]], axis=-1)
    k_mla = jnp.concatenate([k_nope, bcast_heads(t["mla_krope"][:, :, None, :])], axis=-1)
    o_c = blocked_attention([(q_mla, k_mla, v_mla)], MLA_SCALE)
    o_d = blocked_attention([(t["gqa_q"], jnp.repeat(t["gqa_k"], GQA_REP, axis=2),
                              jnp.repeat(t["gqa_v"], GQA_REP, axis=2))], GQA_SCALE)
    x = x + g1 * mixer_output(o_a, t["fnet_u"], o_c, o_d, lp, lam, lam_init)
    x = x + g2 * sqrelu_mlp(modulate(x, lp["norm_mlp_g"], sh2, sc2), lp)
    state = (t["diff_k"], t["diff_v"], t["mla_ckv"], t["mla_krope"], t["gqa_k"], t["gqa_v"])
    return x, state


def latent_layer(x, mod, cache, lp, lam, lam_init, row, col):
    sh1, sc1, g1, sh2, sc2, g2 = mod
    ck_diff, cv_diff, c_ckv, c_kr, ck_gqa, cv_gqa = cache
    t = mixer_inputs(modulate(x, lp["norm_mix_g"], sh1, sc1), lp)
    o_a = blocked_attention([
        (rope_2d(t["diff_q"], row, col), rope_2d(diff_maps(t["diff_k"]), row, col),
         jnp.repeat(t["diff_v"], DIFF_REP, axis=2)),
        (t["diff_q"], diff_maps(ck_diff), jnp.repeat(cv_diff, DIFF_REP, axis=2))], DIFF_SCALE)
    k_nope, v_mla = mla_expand(t["mla_ckv"], lp["mla_w_ukv"])
    k_nope_c, v_mla_c = mla_expand(c_ckv, lp["mla_w_ukv"])
    q_rot = jnp.concatenate([t["mla_q_nope"], rope_2d(t["mla_q_rope"], row, col)], axis=-1)
    q_plain = jnp.concatenate([t["mla_q_nope"], t["mla_q_rope"]], axis=-1)
    k_lat = jnp.concatenate([k_nope, bcast_heads(rope_2d(t["mla_krope"][:, :, None, :], row, col))], axis=-1)
    k_ctx = jnp.concatenate([k_nope_c, bcast_heads(c_kr[:, :, None, :])], axis=-1)
    o_c = blocked_attention([(q_rot, k_lat, v_mla), (q_plain, k_ctx, v_mla_c)], MLA_SCALE)
    o_d = blocked_attention([
        (rope_2d(t["gqa_q"], row, col), jnp.repeat(rope_2d(t["gqa_k"], row, col), GQA_REP, axis=2),
         jnp.repeat(t["gqa_v"], GQA_REP, axis=2)),
        (t["gqa_q"], jnp.repeat(ck_gqa, GQA_REP, axis=2), jnp.repeat(cv_gqa, GQA_REP, axis=2))], GQA_SCALE)
    x = x + g1 * mixer_output(o_a, t["fnet_u"], o_c, o_d, lp, lam, lam_init)
    x = x + g2 * sqrelu_mlp(modulate(x, lp["norm_mlp_g"], sh2, sc2), lp)
    return x


def setup_inputs(seed: int = 0) -> dict:
    key = jax.random.key(seed)
    ks = jax.random.split(key, 32)
    f32 = jnp.float32

    def nrm(k, shape, scale=1.0):
        return jax.random.normal(k, shape, f32) * scale

    def gain(k, shape):
        return 1.0 + 0.05 * jax.random.normal(k, shape, f32)

    return {
        "x_prompt": nrm(ks[0], (BATCH, SEQ, D_MODEL)),
        "x_sample": nrm(ks[1], (DEC_BATCH, DEC_SEQ, D_MODEL)),
        "cache_diff_k": nrm(ks[2], (DEC_BATCH, DEPTH, PAST_LEN, DIFF_HEADS, 2 * DIFF_QK)),
        "cache_diff_v": nrm(ks[3], (DEC_BATCH, DEPTH, PAST_LEN, DIFF_HEADS, DIFF_V)),
        "cache_mla_ckv": nrm(ks[4], (DEC_BATCH, DEPTH, PAST_LEN, MLA_KV_RANK)),
        "cache_mla_krope": nrm(ks[5], (DEC_BATCH, DEPTH, PAST_LEN, MLA_ROPE)),
        "cache_gqa_k": nrm(ks[6], (DEC_BATCH, DEPTH, PAST_LEN, GQA_KV_HEADS, GQA_DIM)),
        "cache_gqa_v": nrm(ks[7], (DEC_BATCH, DEPTH, PAST_LEN, GQA_KV_HEADS, GQA_DIM)),
        "c": nrm(ks[8], (DEC_BATCH, D_MODEL)),
        "c_ctx": nrm(ks[9], (D_MODEL,)),
        "norm_mix_g": gain(ks[10], (DEPTH, D_MODEL)),
        "norm_mlp_g": gain(ks[11], (DEPTH, D_MODEL)),
        "ada_w": nrm(ks[12], (DEPTH, D_MODEL, N_MOD * D_MODEL), D_MODEL ** -0.5),
        "ada_b": nrm(ks[13], (DEPTH, N_MOD * D_MODEL), 0.02),
        "w_in": nrm(ks[14], (DEPTH, D_MODEL, IN_W), D_MODEL ** -0.5),
        "diff_lambda": nrm(ks[15], (DEPTH, 4, DIFF_QK), 0.1),
        "diff_subln_g": gain(ks[16], (DEPTH, DIFF_V)),
        "mla_q_norm_g": gain(ks[17], (DEPTH, MLA_Q_RANK)),
        "mla_w_uq": nrm(ks[18], (DEPTH, MLA_Q_RANK, MLA_HEADS * (MLA_NOPE + MLA_ROPE)), MLA_Q_RANK ** -0.5),
        "mla_kv_norm_g": gain(ks[19], (DEPTH, MLA_KV_RANK)),
        "mla_w_ukv": nrm(ks[20], (DEPTH, MLA_KV_RANK, MLA_HEADS * (MLA_NOPE + MLA_V)), MLA_KV_RANK ** -0.5),
        "gqa_q_norm_g": gain(ks[21], (DEPTH, GQA_DIM)),
        "gqa_k_norm_g": gain(ks[22], (DEPTH, GQA_DIM)),
        "w_out": nrm(ks[23], (DEPTH, MIX_W, D_MODEL), MIX_W ** -0.5),
        "mlp_w1": nrm(ks[24], (DEPTH, D_MODEL, D_FF), D_MODEL ** -0.5),
        "mlp_w2": nrm(ks[25], (DEPTH, D_FF, D_MODEL), D_FF ** -0.5),
        "final_norm_g": gain(ks[26], (D_MODEL,)),
    }


def reference(x_prompt, x_sample, cache_diff_k, cache_diff_v, cache_mla_ckv, cache_mla_krope,
              cache_gqa_k, cache_gqa_v, c, c_ctx, norm_mix_g, norm_mlp_g, ada_w, ada_b, w_in,
              diff_lambda, diff_subln_g, mla_q_norm_g, mla_w_uq, mla_kv_norm_g, mla_w_ukv,
              gqa_q_norm_g, gqa_k_norm_g, w_out, mlp_w1, mlp_w2, final_norm_g):
    n_lat = x_sample.shape[1]
    rows = n_lat // GRID_W
    row = jnp.repeat(jnp.arange(rows, dtype=jnp.int32), GRID_W)
    col = jnp.tile(jnp.arange(GRID_W, dtype=jnp.int32), rows)

    xp = x_prompt
    xs = x_sample
    st_diff_k, st_diff_v, st_ckv, st_kr, st_gk, st_gv = [], [], [], [], [], []
    for l in range(DEPTH):
        lp = {
            "norm_mix_g": norm_mix_g[l], "norm_mlp_g": norm_mlp_g[l], "w_in": w_in[l],
            "diff_subln_g": diff_subln_g[l], "mla_q_norm_g": mla_q_norm_g[l], "mla_w_uq": mla_w_uq[l],
            "mla_kv_norm_g": mla_kv_norm_g[l], "mla_w_ukv": mla_w_ukv[l],
            "gqa_q_norm_g": gqa_q_norm_g[l], "gqa_k_norm_g": gqa_k_norm_g[l],
            "w_out": w_out[l], "mlp_w1": mlp_w1[l], "mlp_w2": mlp_w2[l],
        }
        lam_init = 0.8 - 0.6 * math.exp(-0.3 * l)
        lamp = diff_lambda[l].astype(jnp.float32)
        lam = jnp.exp(jnp.sum(lamp[0] * lamp[1])) - jnp.exp(jnp.sum(lamp[2] * lamp[3])) + lam_init
        mod_ctx = adaln(c_ctx[None, None, :], ada_w[l], ada_b[l])
        mod_lat = adaln(c[:, None, :], ada_w[l], ada_b[l])

        xp, st = context_layer(xp, mod_ctx, lp, lam, lam_init)
        st_diff_k.append(st[0]); st_diff_v.append(st[1]); st_ckv.append(st[2])
        st_kr.append(st[3]); st_gk.append(st[4]); st_gv.append(st[5])

        cache_l = (cache_diff_k[:, l], cache_diff_v[:, l], cache_mla_ckv[:, l], cache_mla_krope[:, l],
                   cache_gqa_k[:, l], cache_gqa_v[:, l])
        xs = latent_layer(xs, mod_lat, cache_l, lp, lam, lam_init, row, col)

    y_prompt = rmsnorm(xp, final_norm_g)
    y_sample = rmsnorm(xs, final_norm_g)
    new_diff_k = jnp.stack(st_diff_k, axis=1)
    new_diff_v = jnp.stack(st_diff_v, axis=1)
    new_mla_ckv = jnp.stack(st_ckv, axis=1)
    new_mla_krope = jnp.stack(st_kr, axis=1)
    new_gqa_k = jnp.stack(st_gk, axis=1)
    new_gqa_v = jnp.stack(st_gv, axis=1)
    return (y_prompt, y_sample, new_diff_k, new_diff_v, new_mla_ckv, new_mla_krope, new_gqa_k, new_gqa_v)
```

```python
import functools
import math

import jax
import jax.numpy as jnp
from jax import lax
from jax.experimental import pallas as pl
from jax.experimental.pallas import tpu as pltpu

F32 = jnp.float32
BF16 = jnp.bfloat16

D_MODEL = 1024
GRID_W = 64
GROUP_W = 256
DIFF_HEADS = 4
DIFF_QK = 32
DIFF_V = 64
FNET_GROUPS = 4
FNET_CH = 64
MLA_HEADS = 4
MLA_Q_RANK = 192
MLA_KV_RANK = 128
MLA_NOPE = 64
MLA_ROPE = 32
MLA_V = 64
MLA_HEAD_W = 128
GQA_HEADS = 4
GQA_KV_HEADS = 2
GQA_DIM = 64
D_FF = 4 * D_MODEL
N_MOD = 6
ROPE_THETA = 10000.0
EPS = 1e-6
DIFF_SCALE = DIFF_QK ** -0.5
MLA_SCALE = (MLA_NOPE + MLA_ROPE) ** -0.5
GQA_SCALE = GQA_DIM ** -0.5

V7X_VMEM_LIMIT_BYTES = 56 * 1024 * 1024

C_AQ, C_AK, C_AV, C_BU, C_CQ, C_CKV, C_CKR, C_DQ, C_DK, C_DV, IN_W_PAD = (
    0, 256, 512, 768, 1024, 1280, 1408, 1536, 1792, 1920, 2048)
KR_LANE = MLA_NOPE

TOKEN_BLOCK = 512
KEY_CHUNK = 512
Q_TILE = 256
MLP_CHUNK = 1024
ADA_TILE = 1536
FOURIER_ROWS = 256


def _params(n_axes):
    return pltpu.CompilerParams(dimension_semantics=("arbitrary",) * n_axes,
                                vmem_limit_bytes=V7X_VMEM_LIMIT_BYTES)


def _dot(a, b):
    return jnp.dot(a, b, preferred_element_type=F32)


def _dot_nt(a, b):
    return lax.dot_general(a, b, (((1,), (1,)), ((), ())), preferred_element_type=F32)


def _adaln_kernel(c_ref, w_ref, b_ref, o_ref):
    c = c_ref[...]
    s = c / (1.0 + jnp.exp(-c))
    o_ref[...] = _dot(s.astype(BF16), w_ref[...].astype(BF16)) + b_ref[...]


def _adaln(cs, ada_w, ada_b):
    depth = ada_w.shape[0]
    rows = cs.shape[0]
    n_out = ada_w.shape[2]
    return pl.pallas_call(
        _adaln_kernel,
        grid=(depth, n_out // ADA_TILE),
        in_specs=[
            pl.BlockSpec((rows, D_MODEL), lambda l, j: (0, 0)),
            pl.BlockSpec((None, D_MODEL, ADA_TILE), lambda l, j: (l, 0, j)),
            pl.BlockSpec((None, 1, ADA_TILE), lambda l, j: (l, 0, j)),
        ],
        out_specs=pl.BlockSpec((None, rows, ADA_TILE), lambda l, j: (l, 0, j)),
        out_shape=jax.ShapeDtypeStruct((depth, rows, n_out), F32),
        compiler_params=_params(2),
        name="adaln",
    )(cs, ada_w, ada_b.reshape(depth, 1, n_out))


def _rms_full(x, g, n):
    ss = jnp.sum(x * x, axis=-1, keepdims=True)
    return x * lax.rsqrt(ss * (1.0 / n) + EPS) * g


def _group_sumsq(x, ones_bd):
    sq = x * x
    hi = sq.astype(BF16)
    lo = (sq - hi.astype(F32)).astype(BF16)
    return _dot(hi, ones_bd) + _dot(lo, ones_bd)


def _rope(x, cos, ssin, half):
    width = x.shape[-1]
    lane = lax.broadcasted_iota(jnp.int32, x.shape, 1)
    first = (lane & half) == 0
    partner = jnp.where(first, pltpu.roll(x, width - half, 1), pltpu.roll(x, half, 1))
    return x * cos + partner * ssin


def _pre_kernel(latent, n_sub, sub_rows, *refs):
    it = iter(refs)
    x_ref, mod_ref, gmix_ref, win_ref = next(it), next(it), next(it), next(it)
    gq_ref, wuq_ref, gkv_ref, wukv_ref = next(it), next(it), next(it), next(it)
    gqn_ref, gkn_ref, onesq_ref, onesk_ref, dup_ref, csc_ref = (
        next(it), next(it), next(it), next(it), next(it), next(it))
    if latent:
        cosd_ref, sind_ref, cosm_ref, sinm_ref, cosg_ref, sing_ref = (
            next(it), next(it), next(it), next(it), next(it), next(it))
    qd_ref, kd_ref, vdt_ref = next(it), next(it), next(it)
    qm_ref, km_ref, vmt_ref = next(it), next(it), next(it)
    qg_ref, kg_ref, vgt_ref = next(it), next(it), next(it)
    ucs_ref = next(it)
    if latent:
        qdr_ref, qmr_ref, qgr_ref = next(it), next(it), next(it)
    else:
        sdk_ref, sdv_ref, sckv_ref, skr_ref, sgk_ref, sgv_ref = (
            next(it), next(it), next(it), next(it), next(it), next(it))

    x = x_ref[...]
    mod = mod_ref[...]
    shift = mod[:, 0:D_MODEL]
    scale = mod[:, D_MODEL:2 * D_MODEL]
    h = _rms_full(x, gmix_ref[...], D_MODEL) * (1.0 + scale) + shift
    z = _dot(h.astype(BF16), win_ref[...])

    def store_t(ref, val):
        for s in range(n_sub):
            ref[s, 0] = val[s * sub_rows:(s + 1) * sub_rows, :].T.astype(BF16)

    a_q = z[:, C_AQ:C_AQ + GROUP_W]
    a_k = z[:, C_AK:C_AK + GROUP_W]
    a_v = z[:, C_AV:C_AV + GROUP_W]
    qd_ref[...] = (a_q * DIFF_SCALE).astype(BF16)
    if latent:
        cosd, sind = cosd_ref[...], sind_ref[...]
        qdr_ref[...] = (_rope(a_q, cosd, sind, DIFF_QK // 4) * DIFF_SCALE).astype(BF16)
        kd_ref[...] = _rope(a_k, cosd, sind, DIFF_QK // 4).astype(BF16)
    else:
        kd_ref[...] = a_k.astype(BF16)
        sdk_ref[...] = a_k
        sdv_ref[...] = a_v
    store_t(vdt_ref, a_v)

    ucs_ref[...] = _dot(z[:, C_BU:C_BU + GROUP_W].astype(BF16), csc_ref[...]).astype(BF16)

    cqn = _rms_full(z[:, C_CQ:C_CQ + 256], gq_ref[...], MLA_Q_RANK)
    q_m = _dot(cqn.astype(BF16), wuq_ref[...])
    qm_ref[...] = (q_m * MLA_SCALE).astype(BF16)
    ckv = _rms_full(z[:, C_CKV:C_CKV + MLA_KV_RANK], gkv_ref[...], MLA_KV_RANK)
    kv = _dot(ckv.astype(BF16), wukv_ref[...])
    kr = z[:, C_CKR:C_CKR + MLA_HEAD_W]
    if latent:
        cosm, sinm = cosm_ref[...], sinm_ref[...]
        for hd in range(MLA_HEADS):
            sl = slice(hd * MLA_HEAD_W, (hd + 1) * MLA_HEAD_W)
            qmr_ref[:, sl] = (_rope(q_m[:, sl], cosm, sinm, MLA_ROPE // 4) * MLA_SCALE).astype(BF16)
        kr = _rope(kr, cosm, sinm, MLA_ROPE // 4)
    else:
        sckv_ref[...] = ckv
        skr_ref[...] = kr[:, KR_LANE:KR_LANE + MLA_ROPE]
    for hd in range(MLA_HEADS):
        sl = slice(hd * MLA_HEAD_W, (hd + 1) * MLA_HEAD_W)
        km_ref[:, sl] = (kv[:, sl] + kr).astype(BF16)
    store_t(vmt_ref, kv[:, MLA_HEADS * MLA_HEAD_W:])

    d_q = z[:, C_DQ:C_DQ + GROUP_W]
    qn = d_q * lax.rsqrt(_group_sumsq(d_q, onesq_ref[...]) * (1.0 / GQA_DIM) + EPS) * gqn_ref[...]
    d_k = z[:, C_DK:C_DK + GQA_KV_HEADS * GQA_DIM]
    kn = d_k * lax.rsqrt(_group_sumsq(d_k, onesk_ref[...]) * (1.0 / GQA_DIM) + EPS) * gkn_ref[...]
    d_v = z[:, C_DV:C_DV + GQA_KV_HEADS * GQA_DIM]
    qg_ref[...] = (qn * GQA_SCALE).astype(BF16)
    if latent:
        cosg, sing = cosg_ref[...], sing_ref[...]
        qgr_ref[...] = (_rope(qn, cosg, sing, GQA_DIM // 4) * GQA_SCALE).astype(BF16)
        k_att = _rope(kn, cosg[:, :GQA_KV_HEADS * GQA_DIM], sing[:, :GQA_KV_HEADS * GQA_DIM], GQA_DIM // 4)
    else:
        sgk_ref[...] = kn
        sgv_ref[...] = d_v
        k_att = kn
    kg_ref[...] = _dot(k_att.astype(BF16), dup_ref[...]).astype(BF16)
    store_t(vgt_ref, d_v)


def _pre(x2d, mods3, mod_row_fn, lw, consts, seq, latent, rope_tabs=None):
    n_tok = x2d.shape[0]
    tb = TOKEN_BLOCK
    n_blk = n_tok // tb
    n_bat = n_tok // seq
    if seq >= tb:
        n_sub, sub_rows = 1, tb
        blk_per_seq = seq // tb
        vt_shape = lambda r: (n_bat, blk_per_seq, r, tb)
        vt_spec = lambda r: pl.BlockSpec((1, 1, r, tb), lambda i: (i // blk_per_seq, i % blk_per_seq, 0, 0))
    else:
        n_sub, sub_rows = tb // seq, seq
        blk_per_seq = 1
        vt_shape = lambda r: (n_bat, 1, r, seq)
        vt_spec = lambda r: pl.BlockSpec((n_sub, 1, r, seq), lambda i: (i, 0, 0, 0))

    def full(a):
        nd = a.ndim
        return pl.BlockSpec(a.shape, lambda i, _n=nd: (0,) * _n)

    tok = lambda w: pl.BlockSpec((tb, w), lambda i: (i, 0))
    tok_shape = lambda w, dt: jax.ShapeDtypeStruct((n_tok, w), dt)

    inputs = [x2d, mods3, lw["gmix"], lw["w_in"], lw["gq"], lw["w_uq"], lw["gkv"], lw["w_ukv"],
              lw["gqn"], lw["gkn"], consts["ones_q"], consts["ones_k"], consts["dup"], consts["csc"]]
    in_specs = [tok(D_MODEL),
                pl.BlockSpec((None, 1, N_MOD * D_MODEL), lambda i: (mod_row_fn(i), 0, 0))]
    in_specs += [full(a) for a in inputs[2:]]
    if latent:
        for t in rope_tabs:
            inputs.append(t)
            in_specs.append(pl.BlockSpec((tb, t.shape[1]), lambda i: (i % blk_per_seq, 0)))

    names = ["qd", "kd", "vdt", "qm", "km", "vmt", "qg", "kg", "vgt", "ucs"]
    out_shape = [tok_shape(256, BF16), tok_shape(256, BF16), jax.ShapeDtypeStruct(vt_shape(256), BF16),
                 tok_shape(512, BF16), tok_shape(512, BF16), jax.ShapeDtypeStruct(vt_shape(256), BF16),
                 tok_shape(256, BF16), tok_shape(256, BF16), jax.ShapeDtypeStruct(vt_shape(128), BF16),
                 tok_shape(512, BF16)]
    out_specs = [tok(256), tok(256), vt_spec(256), tok(512), tok(512), vt_spec(256),
                 tok(256), tok(256), vt_spec(128), tok(512)]
    if latent:
        names += ["qd_rot", "qm_rot", "qg_rot"]
        out_shape += [tok_shape(256, BF16), tok_shape(512, BF16), tok_shape(256, BF16)]
        out_specs += [tok(256), tok(512), tok(256)]
    else:
        names += ["s_diff_k", "s_diff_v", "s_ckv", "s_kr", "s_gqa_k", "s_gqa_v"]
        widths = [256, 256, MLA_KV_RANK, MLA_ROPE, 128, 128]
        out_shape += [tok_shape(w, F32) for w in widths]
        out_specs += [tok(w) for w in widths]

    outs = pl.pallas_call(
        functools.partial(_pre_kernel, latent, n_sub, sub_rows),
        grid=(n_blk,),
        in_specs=in_specs,
        out_specs=out_specs,
        out_shape=out_shape,
        compiler_params=_params(1),
        name="pre_latent" if latent else "pre_context",
    )(*inputs)
    return dict(zip(names, outs))


def _cache_kernel(dk_ref, dv_ref, ckv_ref, kr_ref, gk_ref, gv_ref, wukv_ref, place_ref, dup_ref,
                  kd_ref, vdt_ref, km_ref, vmt_ref, kg_ref, vgt_ref):
    kd_ref[...] = dk_ref[...].astype(BF16)
    vdt_ref[...] = dv_ref[...].T.astype(BF16)
    kv = _dot(ckv_ref[...].astype(BF16), wukv_ref[...])
    kr4 = _dot(kr_ref[...].astype(BF16), place_ref[...])
    n_k = MLA_HEADS * MLA_HEAD_W
    km_ref[...] = (kv[:, :n_k] + kr4).astype(BF16)
    vmt_ref[...] = kv[:, n_k:].T.astype(BF16)
    kg_ref[...] = _dot(gk_ref[...].astype(BF16), dup_ref[...]).astype(BF16)
    vgt_ref[...] = gv_ref[...].T.astype(BF16)


def _cache_prep(cache_diff_k, cache_diff_v, cache_mla_ckv, cache_mla_krope, cache_gqa_k, cache_gqa_v,
                w_ukv_all, consts):
    db, depth, past = cache_diff_k.shape[:3]
    ins = [cache_diff_k.reshape(db, depth, past, 256), cache_diff_v.reshape(db, depth, past, 256),
           cache_mla_ckv, cache_mla_krope, cache_gqa_k.reshape(db, depth, past, 128),
           cache_gqa_v.reshape(db, depth, past, 128)]
    cspec = lambda w: pl.BlockSpec((None, None, past, w), lambda l, b: (b, l, 0, 0))
    in_specs = [cspec(a.shape[-1]) for a in ins]
    in_specs += [pl.BlockSpec((None,) + w_ukv_all.shape[1:], lambda l, b: (l, 0, 0)),
                 pl.BlockSpec(consts["place_kr"].shape, lambda l, b: (0, 0)),
                 pl.BlockSpec(consts["dup"].shape, lambda l, b: (0, 0))]
    kspec = lambda w: pl.BlockSpec((None, None, past, w), lambda l, b: (l, b, 0, 0))
    vspec = lambda r: pl.BlockSpec((None, None, r, past), lambda l, b: (l, b, 0, 0))
    kshape = lambda w: jax.ShapeDtypeStruct((depth, db, past, w), BF16)
    vshape = lambda r: jax.ShapeDtypeStruct((depth, db, r, past), BF16)
    outs = pl.pallas_call(
        _cache_kernel,
        grid=(depth, db),
        in_specs=in_specs,
        out_specs=[kspec(256), vspec(256), kspec(512), vspec(256), kspec(256), vspec(128)],
        out_shape=[kshape(256), vshape(256), kshape(512), vshape(256), kshape(256), vshape(128)],
        compiler_params=_params(2),
        name="cache_prep",
    )(*ins, w_ukv_all, consts["place_kr"], consts["dup"])
    return dict(zip(["kd", "vdt", "km", "vmt", "kg", "vgt"], outs))


def _attn_kernel(group, n_parts, chunk_counts, lam_init, *refs):
    it = iter(refs)
    q_refs = [next(it) for _ in range(n_parts)]
    kv_refs = [(next(it), next(it)) for _ in range(n_parts)]
    if group == "diff":
        lam_ref, g_ref = next(it), next(it)
    o_ref = next(it)

    n_heads = {"diff": 2 * DIFF_HEADS, "mla": MLA_HEADS, "gqa": GQA_HEADS}[group]
    tq = o_ref.shape[0]
    qs = [r[...] for r in q_refs]
    if group != "mla":
        lane = lax.broadcasted_iota(jnp.int32, qs[0].shape, 1)
        head_w = qs[0].shape[1] // n_heads

    outs = []
    for hd in range(n_heads):
        if group == "mla":
            k_sl = slice(hd * MLA_HEAD_W, (hd + 1) * MLA_HEAD_W)
            q_eff = [q[:, k_sl] for q in qs]
            v_row = hd * MLA_V
        else:
            k_sl = slice(None)
            keep = (lane >= hd * head_w) & (lane < (hd + 1) * head_w)
            q_eff = [jnp.where(keep, q.astype(F32), 0.0).astype(BF16) for q in qs]
            v_row = (hd // 2) * 64

        def update(carry, k_blk, vt_blk, q_e):
            m, l, acc = carry
            s = _dot_nt(k_blk, q_e)
            m_new = jnp.maximum(m, jnp.max(s, axis=0, keepdims=True))
            alpha = jnp.exp(m - m_new)
            p = jnp.exp(s - m_new)
            l = alpha * l + jnp.sum(p, axis=0, keepdims=True)
            acc = alpha * acc + _dot(vt_blk, p.astype(BF16))
            return m_new, l, acc

        carry = (jnp.full((1, tq), -jnp.inf, F32), jnp.zeros((1, tq), F32), jnp.zeros((64, tq), F32))
        for part in range(n_parts):
            k_ref, vt_ref = kv_refs[part]
            n_chunks = chunk_counts[part]
            chunk = k_ref.shape[0] // n_chunks
            if n_chunks == 1:
                carry = update(carry, k_ref[:, k_sl], vt_ref[0, v_row:v_row + 64, :], q_eff[part])
            else:
                def body(c, cr, k_ref=k_ref, vt_ref=vt_ref, q_e=q_eff[part], chunk=chunk):
                    start = pl.multiple_of(c * chunk, chunk)
                    return update(cr, k_ref[pl.ds(start, chunk), k_sl], vt_ref[c, v_row:v_row + 64, :], q_e)
                carry = lax.fori_loop(0, n_chunks, body, carry)
        _, l, acc = carry
        outs.append(acc / l)

    if group == "diff":
        lp = lam_ref[...]
        lam = (jnp.exp(jnp.sum(lp[0:1] * lp[1:2], axis=-1, keepdims=True))
               - jnp.exp(jnp.sum(lp[2:3] * lp[3:4], axis=-1, keepdims=True)) + lam_init)
        g = g_ref[...]
        comb = []
        for hd in range(DIFF_HEADS):
            y = outs[2 * hd] - lam * outs[2 * hd + 1]
            ms = jnp.sum(y * y, axis=0, keepdims=True) * (1.0 / DIFF_V)
            comb.append(y * lax.rsqrt(ms + EPS) * g * (1.0 - lam_init))
        outs = comb
    o_ref[...] = jnp.concatenate(outs, axis=0).T.astype(o_ref.dtype)


def _attention(group, q_list, kv_list, n_bat, seq, lam_init=0.0, extras=()):
    tq = min(Q_TILE, seq)
    n_q = seq // tq
    in_specs, inputs, chunk_counts = [], [], []
    for q in q_list:
        inputs.append(q)
        in_specs.append(pl.BlockSpec((tq, q.shape[1]), lambda b, i: (b * n_q + i, 0)))
    for k, vt, kspec, vspec, n_chunks in kv_list:
        inputs += [k, vt]
        in_specs += [kspec, vspec]
        chunk_counts.append(n_chunks)
    for e in extras:
        inputs.append(e)
        in_specs.append(pl.BlockSpec(e.shape, lambda b, i, _n=e.ndim: (0,) * _n))
    return pl.pallas_call(
        functools.partial(_attn_kernel, group, len(q_list), tuple(chunk_counts), lam_init),
        grid=(n_bat, n_q),
        in_specs=in_specs,
        out_specs=pl.BlockSpec((tq, GROUP_W), lambda b, i: (b * n_q + i, 0)),
        out_shape=jax.ShapeDtypeStruct((n_bat * seq, GROUP_W), BF16),
        compiler_params=_params(2),
        name="attn_" + group,
    )(*inputs)


def _own_kv(k, vt, seq):
    n_chunks = vt.shape[1]
    kspec = pl.BlockSpec((seq, k.shape[1]), lambda b, i: (b, 0))
    vspec = pl.BlockSpec((None,) + vt.shape[1:], lambda b, i: (b, 0, 0, 0))
    return (k, vt, kspec, vspec, n_chunks)


def _cache_kv(k, vt, layer):
    past = k.shape[2]
    vt4 = vt.reshape(vt.shape[0], vt.shape[1], 1, vt.shape[2], past)
    kspec = pl.BlockSpec((None, None, past, k.shape[3]), lambda b, i: (layer, b, 0, 0))
    vspec = pl.BlockSpec((None, None, 1, vt.shape[2], past), lambda b, i: (layer, b, 0, 0, 0))
    return (k, vt4, kspec, vspec, 1)


def _fourier_kernel(n_bat, cos_ref, sin_ref, ucs_ref, o_ref):
    cs = cos_ref[...]
    sn = sin_ref[...]
    for b in range(n_bat):
        u = ucs_ref[b]
        y = _dot(cs, u[:, :GROUP_W]) + _dot(sn, u[:, GROUP_W:])
        o_ref[b] = y.astype(o_ref.dtype)


def _fourier(ucs, n_bat, seq, cos_tab, sin_tab):
    tm = min(FOURIER_ROWS, seq)
    u3 = ucs.reshape(n_bat, seq, 2 * GROUP_W)
    out = pl.pallas_call(
        functools.partial(_fourier_kernel, n_bat),
        grid=(seq // tm,),
        in_specs=[pl.BlockSpec((tm, seq), lambda i: (i, 0)),
                  pl.BlockSpec((tm, seq), lambda i: (i, 0)),
                  pl.BlockSpec((n_bat, seq, 2 * GROUP_W), lambda i: (0, 0, 0))],
        out_specs=pl.BlockSpec((n_bat, tm, GROUP_W), lambda i: (0, i, 0)),
        out_shape=jax.ShapeDtypeStruct((n_bat, seq, GROUP_W), BF16),
        compiler_params=_params(1),
        name="fourier",
    )(cos_tab, sin_tab, u3)
    return out.reshape(n_bat * seq, GROUP_W)


def _post_kernel(final, x_ref, od_ref, yf_ref, om_ref, og_ref, mod_ref, wout_ref, gmlp_ref,
                 w1_ref, w2_ref, gfin_ref, o_ref):
    mod = mod_ref[...]
    gate1 = mod[:, 2 * D_MODEL:3 * D_MODEL]
    shift2 = mod[:, 3 * D_MODEL:4 * D_MODEL]
    scale2 = mod[:, 4 * D_MODEL:5 * D_MODEL]
    gate2 = mod[:, 5 * D_MODEL:6 * D_MODEL]
    mix = (_dot(od_ref[...], wout_ref[0]) + _dot(yf_ref[...], wout_ref[1])
           + _dot(om_ref[...], wout_ref[2]) + _dot(og_ref[...], wout_ref[3]))
    x1 = x_ref[...] + gate1 * mix
    h = (_rms_full(x1, gmlp_ref[...], D_MODEL) * (1.0 + scale2) + shift2).astype(BF16)
    acc = jnp.zeros(x1.shape, F32)
    for c in range(D_FF // MLP_CHUNK):
        sl = slice(c * MLP_CHUNK, (c + 1) * MLP_CHUNK)
        a = jnp.maximum(_dot(h, w1_ref[:, sl]), 0.0)
        acc = acc + _dot((a * a).astype(BF16), w2_ref[sl, :])
    x2 = x1 + gate2 * acc
    if final:
        x2 = _rms_full(x2, gfin_ref[...], D_MODEL)
    o_ref[...] = x2


def _post(x2d, o_diff, y_f, o_mla, o_gqa, mods3, mod_row_fn, lw, g_final, final, name):
    n_tok = x2d.shape[0]
    tb = TOKEN_BLOCK
    tok = lambda w: pl.BlockSpec((tb, w), lambda i: (i, 0))
    const = lambda a: pl.BlockSpec(a.shape, lambda i, _n=a.ndim: (0,) * _n)
    return pl.pallas_call(
        functools.partial(_post_kernel, final),
        grid=(n_tok // tb,),
        in_specs=[tok(D_MODEL), tok(GROUP_W), tok(GROUP_W), tok(GROUP_W), tok(GROUP_W),
                  pl.BlockSpec((None, 1, N_MOD * D_MODEL), lambda i: (mod_row_fn(i), 0, 0)),
                  const(lw["w_out"]), const(lw["gmlp"]), const(lw["w1"]), const(lw["w2"]), const(g_final)],
        out_specs=tok(D_MODEL),
        out_shape=jax.ShapeDtypeStruct((n_tok, D_MODEL), F32),
        compiler_params=_params(1),
        name=name,
    )(x2d, o_diff, y_f, o_mla, o_gqa, mods3, lw["w_out"], lw["gmlp"], lw["w1"], lw["w2"], g_final)


def _rope_tables(n_lat, head_dim, width, lane0=0):
    a = head_dim // 2
    inv = ROPE_THETA ** (-jnp.arange(0, a, 2, dtype=F32) / a)
    pos = jnp.arange(n_lat, dtype=jnp.int32)
    row = (pos // GRID_W).astype(F32)
    col = (pos % GRID_W).astype(F32)
    lane = jnp.arange(width, dtype=jnp.int32) - lane0
    in_pat = (lane >= 0) & ((lane < head_dim) | (lane0 == 0))
    li = jnp.where(in_pat, lane % head_dim, 0)
    part = li // a
    ii = li % a
    first = (ii // (a // 2)) == 0
    f = ii % (a // 2)
    p = jnp.where(part[None, :] == 0, row[:, None], col[:, None])
    ang = p * inv[f][None, :]
    cos = jnp.where(in_pat[None, :], jnp.cos(ang), 1.0)
    ssin = jnp.where(in_pat[None, :], jnp.where(first[None, :], -jnp.sin(ang), jnp.sin(ang)), 0.0)
    return cos, ssin


def _dft_tables(n):
    j = jnp.arange(n, dtype=jnp.int32)
    m = (j[:, None] * j[None, :]) % n
    ang = m.astype(F32) * (2.0 * math.pi / n)
    s = n ** -0.5
    return (jnp.cos(ang) * s).astype(BF16), (jnp.sin(ang) * s).astype(BF16)


def _constants():
    eye = lambda n: jnp.eye(n, dtype=F32)
    ones_q = jnp.kron(eye(GQA_HEADS), jnp.ones((GQA_DIM, GQA_DIM), F32)).astype(BF16)
    ones_k = jnp.kron(eye(GQA_KV_HEADS), jnp.ones((GQA_DIM, GQA_DIM), F32)).astype(BF16)
    dup = jnp.kron(eye(GQA_KV_HEADS), jnp.concatenate([eye(GQA_DIM), eye(GQA_DIM)], axis=1)).astype(BF16)
    one_head = jnp.concatenate([jnp.zeros((MLA_ROPE, KR_LANE), F32), eye(MLA_ROPE),
                                jnp.zeros((MLA_ROPE, MLA_HEAD_W - KR_LANE - MLA_ROPE), F32)], axis=1)
    place_kr = jnp.tile(one_head, (1, MLA_HEADS)).astype(BF16)
    j = jnp.arange(FNET_CH, dtype=jnp.int32)
    ang = ((j[:, None] * j[None, :]) % FNET_CH).astype(F32) * (2.0 * math.pi / FNET_CH)
    cc = jnp.kron(eye(FNET_GROUPS), jnp.cos(ang) * FNET_CH ** -0.5)
    sc = jnp.kron(eye(FNET_GROUPS), jnp.sin(ang) * FNET_CH ** -0.5)
    csc = jnp.concatenate([cc, -sc], axis=1).astype(BF16)
    return dict(ones_q=ones_q, ones_k=ones_k, dup=dup, place_kr=place_kr, csc=csc)


def _layer_weights(l, norm_mix_g, norm_mlp_g, w_in, mla_q_norm_g, mla_w_uq, mla_kv_norm_g, mla_w_ukv,
                   gqa_q_norm_g, gqa_k_norm_g, w_out, mlp_w1, mlp_w2):
    wi = w_in[l]
    zc = lambda n: jnp.zeros((D_MODEL, n), F32)
    off = [0, 256, 512, 768, 1024, 1216, 1344, 1376, 1632, 1760, 1888]
    seg = [wi[:, off[k]:off[k + 1]] for k in range(10)]
    w_in_p = jnp.concatenate([
        seg[0], seg[1], seg[2], seg[3],
        seg[4], zc(256 - MLA_Q_RANK),
        seg[5],
        zc(KR_LANE), seg[6], zc(MLA_HEAD_W - KR_LANE - MLA_ROPE),
        seg[7], seg[8], seg[9]], axis=1).astype(BF16)
    wuq = mla_w_uq[l].reshape(MLA_Q_RANK, MLA_HEADS, MLA_NOPE + MLA_ROPE)
    wuq = jnp.pad(wuq, ((0, 256 - MLA_Q_RANK), (0, 0), (0, MLA_HEAD_W - MLA_NOPE - MLA_ROPE)))
    w_uq_p = wuq.reshape(256, MLA_HEADS * MLA_HEAD_W).astype(BF16)
    wukv = mla_w_ukv[l].reshape(MLA_KV_RANK, MLA_HEADS, MLA_NOPE + MLA_V)
    wk = jnp.pad(wukv[:, :, :MLA_NOPE], ((0, 0), (0, 0), (0, MLA_HEAD_W - MLA_NOPE)))
    wv = wukv[:, :, MLA_NOPE:]
    w_ukv_p = jnp.concatenate([wk.reshape(MLA_KV_RANK, -1), wv.reshape(MLA_KV_RANK, -1)], axis=1).astype(BF16)
    return dict(
        gmix=norm_mix_g[l][None, :], gmlp=norm_mlp_g[l][None, :], w_in=w_in_p,
        gq=jnp.pad(mla_q_norm_g[l], (0, 256 - MLA_Q_RANK))[None, :], w_uq=w_uq_p,
        gkv=mla_kv_norm_g[l][None, :], w_ukv=w_ukv_p,
        gqn=jnp.tile(gqa_q_norm_g[l], GQA_HEADS)[None, :], gkn=jnp.tile(gqa_k_norm_g[l], GQA_KV_HEADS)[None, :],
        w_out=w_out[l].reshape(4, GROUP_W, D_MODEL).astype(BF16),
        w1=mlp_w1[l].astype(BF16), w2=mlp_w2[l].astype(BF16))


def kernel(x_prompt, x_sample, cache_diff_k, cache_diff_v, cache_mla_ckv, cache_mla_krope, cache_gqa_k, cache_gqa_v, c, c_ctx, norm_mix_g, norm_mlp_g, ada_w, ada_b, w_in, diff_lambda, diff_subln_g, mla_q_norm_g, mla_w_uq, mla_kv_norm_g, mla_w_ukv, gqa_q_norm_g, gqa_k_norm_g, w_out, mlp_w1, mlp_w2, final_norm_g):
    n_ctx_b, seq, _ = x_prompt.shape
    n_lat_b, n_lat, _ = x_sample.shape
    depth = w_in.shape[0]
    assert seq % 128 == 0 and TOKEN_BLOCK % seq == 0 and (n_ctx_b * seq) % TOKEN_BLOCK == 0
    assert n_lat % TOKEN_BLOCK == 0 and n_lat % KEY_CHUNK == 0 and n_lat % GRID_W == 0

    consts = _constants()
    rows = 1 + n_lat_b
    rows_pad = -(-rows // 8) * 8
    cs = jnp.concatenate([c_ctx[None, :], c, jnp.zeros((rows_pad - rows, D_MODEL), F32)], axis=0)
    mods = _adaln(cs, ada_w, ada_b).reshape(depth * rows_pad, 1, N_MOD * D_MODEL)

    rope_tabs = (_rope_tables(n_lat, DIFF_QK, GROUP_W) + _rope_tables(n_lat, MLA_ROPE, MLA_HEAD_W, KR_LANE)
                 + _rope_tables(n_lat, GQA_DIM, GROUP_W))
    dft_ctx = _dft_tables(seq)
    dft_lat = _dft_tables(n_lat)

    layers = [_layer_weights(l, norm_mix_g, norm_mlp_g, w_in, mla_q_norm_g, mla_w_uq, mla_kv_norm_g,
                             mla_w_ukv, gqa_q_norm_g, gqa_k_norm_g, w_out, mlp_w1, mlp_w2)
              for l in range(depth)]
    cache = _cache_prep(cache_diff_k, cache_diff_v, cache_mla_ckv, cache_mla_krope, cache_gqa_k, cache_gqa_v,
                        jnp.stack([lw["w_ukv"] for lw in layers]), consts)

    g_final = final_norm_g[None, :]
    xp = x_prompt.reshape(n_ctx_b * seq, D_MODEL)
    xs = x_sample.reshape(n_lat_b * n_lat, D_MODEL)
    lat_blocks = n_lat // TOKEN_BLOCK
    states = [[] for _ in range(6)]
    for l in range(depth):
        lw = layers[l]
        lam_init = 0.8 - 0.6 * math.exp(-0.3 * l)
        ctx_row = lambda i, _l=l: _l * rows_pad
        lat_row = lambda i, _l=l: _l * rows_pad + 1 + i // lat_blocks
        lam_p = diff_lambda[l].astype(F32)
        final = l == depth - 1

        t = _pre(xp, mods, ctx_row, lw, consts, seq, latent=False)
        g_t = jnp.broadcast_to(diff_subln_g[l][:, None], (DIFF_V, min(Q_TILE, seq)))
        o_a = _attention("diff", [t["qd"]], [_own_kv(t["kd"], t["vdt"], seq)], n_ctx_b, seq, lam_init,
                         extras=(lam_p, g_t))
        o_c = _attention("mla", [t["qm"]], [_own_kv(t["km"], t["vmt"], seq)], n_ctx_b, seq)
        o_d = _attention("gqa", [t["qg"]], [_own_kv(t["kg"], t["vgt"], seq)], n_ctx_b, seq)
        y_f = _fourier(t["ucs"], n_ctx_b, seq, *dft_ctx)
        xp = _post(xp, o_a, y_f, o_c, o_d, mods, ctx_row, lw, g_final, final, "post_context")
        for k, nm in enumerate(["s_diff_k", "s_diff_v", "s_ckv", "s_kr", "s_gqa_k", "s_gqa_v"]):
            states[k].append(t[nm])

        t = _pre(xs, mods, lat_row, lw, consts, n_lat, latent=True, rope_tabs=rope_tabs)
        g_t = jnp.broadcast_to(diff_subln_g[l][:, None], (DIFF_V, min(Q_TILE, n_lat)))
        o_a = _attention("diff", [t["qd_rot"], t["qd"]],
                         [_own_kv(t["kd"], t["vdt"], n_lat), _cache_kv(cache["kd"], cache["vdt"], l)],
                         n_lat_b, n_lat, lam_init, extras=(lam_p, g_t))
        o_c = _attention("mla", [t["qm_rot"], t["qm"]],
                         [_own_kv(t["km"], t["vmt"], n_lat), _cache_kv(cache["km"], cache["vmt"], l)],
                         n_lat_b, n_lat)
        o_d = _attention("gqa", [t["qg_rot"], t["qg"]],
                         [_own_kv(t["kg"], t["vgt"], n_lat), _cache_kv(cache["kg"], cache["vgt"], l)],
                         n_lat_b, n_lat)
        y_f = _fourier(t["ucs"], n_lat_b, n_lat, *dft_lat)
        xs = _post(xs, o_a, y_f, o_c, o_d, mods, lat_row, lw, g_final, final, "post_latent")

    y_prompt = xp.reshape(n_ctx_b, seq, D_MODEL)
    y_sample = xs.reshape(n_lat_b, n_lat, D_MODEL)
    st = [jnp.stack([a.reshape(n_ctx_b, seq, -1) for a in s], axis=1) for s in states]
    new_diff_k = st[0].reshape(n_ctx_b, depth, seq, DIFF_HEADS, 2 * DIFF_QK)
    new_diff_v = st[1].reshape(n_ctx_b, depth, seq, DIFF_HEADS, DIFF_V)
    new_gqa_k = st[4].reshape(n_ctx_b, depth, seq, GQA_KV_HEADS, GQA_DIM)
    new_gqa_v = st[5].reshape(n_ctx_b, depth, seq, GQA_KV_HEADS, GQA_DIM)
    return (y_prompt, y_sample, new_diff_k, new_diff_v, st[2], st[3], new_gqa_k, new_gqa_v)
```

```python
import functools
import math

import jax
import jax.numpy as jnp
from jax import lax
from jax.experimental import pallas as pl
from jax.experimental.pallas import tpu as pltpu

F32 = jnp.float32
BF16 = jnp.bfloat16

D_MODEL = 1024
GRID_W = 64
GROUP_W = 256
DIFF_HEADS = 4
DIFF_QK = 32
DIFF_V = 64
FNET_GROUPS = 4
FNET_CH = 64
MLA_HEADS = 4
MLA_Q_RANK = 192
MLA_KV_RANK = 128
MLA_NOPE = 64
MLA_ROPE = 32
MLA_V = 64
MLA_HEAD_W = 128
GQA_HEADS = 4
GQA_KV_HEADS = 2
GQA_DIM = 64
D_FF = 4 * D_MODEL
N_MOD = 6
ROPE_THETA = 10000.0
EPS = 1e-6
DIFF_SCALE = DIFF_QK ** -0.5
MLA_SCALE = (MLA_NOPE + MLA_ROPE) ** -0.5
GQA_SCALE = GQA_DIM ** -0.5

V7X_VMEM_LIMIT_BYTES = 56 * 1024 * 1024

C_AQ, C_AK, C_AV, C_BU, C_CQ, C_CKV, C_CKR, C_DQ, C_DK, C_DV, IN_W_PAD = (
    0, 256, 512, 768, 1024, 1280, 1408, 1536, 1792, 1920, 2048)
KR_LANE = MLA_NOPE

TOKEN_BLOCK = 512
KEY_CHUNK = 512
Q_TILE = 512
MLP_CHUNK = 1024
ADA_TILE = 1536
FOURIER_ROWS = 256


def _params(n_axes):
    return pltpu.CompilerParams(dimension_semantics=("arbitrary",) * n_axes,
                                vmem_limit_bytes=V7X_VMEM_LIMIT_BYTES)


def _dot(a, b):
    return jnp.dot(a, b, preferred_element_type=F32)


def _dot_nt(a, b):
    return lax.dot_general(a, b, (((1,), (1,)), ((), ())), preferred_element_type=F32)


def _adaln_kernel(c_ref, w_ref, b_ref, o_ref):
    c = c_ref[...]
    s = c / (1.0 + jnp.exp(-c))
    o_ref[...] = _dot(s.astype(BF16), w_ref[...].astype(BF16)) + b_ref[...]


def _adaln(cs, ada_w, ada_b):
    depth = ada_w.shape[0]
    rows = cs.shape[0]
    n_out = ada_w.shape[2]
    return pl.pallas_call(
        _adaln_kernel,
        grid=(depth, n_out // ADA_TILE),
        in_specs=[
            pl.BlockSpec((rows, D_MODEL), lambda l, j: (0, 0)),
            pl.BlockSpec((None, D_MODEL, ADA_TILE), lambda l, j: (l, 0, j)),
            pl.BlockSpec((None, 1, ADA_TILE), lambda l, j: (l, 0, j)),
        ],
        out_specs=pl.BlockSpec((None, rows, ADA_TILE), lambda l, j: (l, 0, j)),
        out_shape=jax.ShapeDtypeStruct((depth, rows, n_out), F32),
        compiler_params=_params(2),
        name="adaln",
    )(cs, ada_w, ada_b.reshape(depth, 1, n_out))


def _rms_full(x, g, n):
    ss = jnp.sum(x * x, axis=-1, keepdims=True)
    return x * lax.rsqrt(ss * (1.0 / n) + EPS) * g


def _group_sumsq(x, ones_bd):
    sq = x * x
    hi = sq.astype(BF16)
    lo = (sq - hi.astype(F32)).astype(BF16)
    return _dot(hi, ones_bd) + _dot(lo, ones_bd)


def _rope(x, cos, ssin, half):
    width = x.shape[-1]
    lane = lax.broadcasted_iota(jnp.int32, x.shape, 1)
    first = (lane & half) == 0
    partner = jnp.where(first, pltpu.roll(x, width - half, 1), pltpu.roll(x, half, 1))
    return x * cos + partner * ssin


def _pre_kernel(latent, n_sub, sub_rows, *refs):
    it = iter(refs)
    x_ref, mod_ref, gmix_ref, win_ref = next(it), next(it), next(it), next(it)
    gq_ref, wuq_ref, gkv_ref, wukv_ref = next(it), next(it), next(it), next(it)
    gqn_ref, gkn_ref, onesq_ref, onesk_ref, dup_ref, csc_ref = (
        next(it), next(it), next(it), next(it), next(it), next(it))
    if latent:
        cosd_ref, sind_ref, cosm_ref, sinm_ref, cosg_ref, sing_ref = (
            next(it), next(it), next(it), next(it), next(it), next(it))
    qd_ref, kd_ref, vdt_ref = next(it), next(it), next(it)
    qm_ref, km_ref, vmt_ref = next(it), next(it), next(it)
    qg_ref, kg_ref, vgt_ref = next(it), next(it), next(it)
    ucs_ref = next(it)
    if latent:
        qdr_ref, qmr_ref, qgr_ref = next(it), next(it), next(it)
    else:
        sdk_ref, sdv_ref, sckv_ref, skr_ref, sgk_ref, sgv_ref = (
            next(it), next(it), next(it), next(it), next(it), next(it))

    x = x_ref[...]
    mod = mod_ref[...]
    shift = mod[:, 0:D_MODEL]
    scale = mod[:, D_MODEL:2 * D_MODEL]
    h = _rms_full(x, gmix_ref[...], D_MODEL) * (1.0 + scale) + shift
    z = _dot(h.astype(BF16), win_ref[...])

    def store_t(ref, val):
        for s in range(n_sub):
            ref[s, 0] = val[s * sub_rows:(s + 1) * sub_rows, :].T.astype(BF16)

    a_q = z[:, C_AQ:C_AQ + GROUP_W]
    a_k = z[:, C_AK:C_AK + GROUP_W]
    a_v = z[:, C_AV:C_AV + GROUP_W]
    qd_ref[...] = (a_q * DIFF_SCALE).astype(BF16)
    if latent:
        cosd, sind = cosd_ref[...], sind_ref[...]
        qdr_ref[...] = (_rope(a_q, cosd, sind, DIFF_QK // 4) * DIFF_SCALE).astype(BF16)
        kd_ref[...] = _rope(a_k, cosd, sind, DIFF_QK // 4).astype(BF16)
    else:
        kd_ref[...] = a_k.astype(BF16)
        sdk_ref[...] = a_k
        sdv_ref[...] = a_v
    store_t(vdt_ref, a_v)

    ucs_ref[...] = _dot(z[:, C_BU:C_BU + GROUP_W].astype(BF16), csc_ref[...]).astype(BF16)

    cqn = _rms_full(z[:, C_CQ:C_CQ + 256], gq_ref[...], MLA_Q_RANK)
    q_m = _dot(cqn.astype(BF16), wuq_ref[...])
    qm_ref[...] = (q_m * MLA_SCALE).astype(BF16)
    ckv = _rms_full(z[:, C_CKV:C_CKV + MLA_KV_RANK], gkv_ref[...], MLA_KV_RANK)
    kv = _dot(ckv.astype(BF16), wukv_ref[...])
    kr = z[:, C_CKR:C_CKR + MLA_HEAD_W]
    if latent:
        cosm, sinm = cosm_ref[...], sinm_ref[...]
        for hd in range(MLA_HEADS):
            sl = slice(hd * MLA_HEAD_W, (hd + 1) * MLA_HEAD_W)
            qmr_ref[:, sl] = (_rope(q_m[:, sl], cosm, sinm, MLA_ROPE // 4) * MLA_SCALE).astype(BF16)
        kr = _rope(kr, cosm, sinm, MLA_ROPE // 4)
    else:
        sckv_ref[...] = ckv
        skr_ref[...] = kr[:, KR_LANE:KR_LANE + MLA_ROPE]
    for hd in range(MLA_HEADS):
        sl = slice(hd * MLA_HEAD_W, (hd + 1) * MLA_HEAD_W)
        km_ref[hd] = (kv[:, sl] + kr).astype(BF16)
    store_t(vmt_ref, kv[:, MLA_HEADS * MLA_HEAD_W:])

    d_q = z[:, C_DQ:C_DQ + GROUP_W]
    qn = d_q * lax.rsqrt(_group_sumsq(d_q, onesq_ref[...]) * (1.0 / GQA_DIM) + EPS) * gqn_ref[...]
    d_k = z[:, C_DK:C_DK + GQA_KV_HEADS * GQA_DIM]
    kn = d_k * lax.rsqrt(_group_sumsq(d_k, onesk_ref[...]) * (1.0 / GQA_DIM) + EPS) * gkn_ref[...]
    d_v = z[:, C_DV:C_DV + GQA_KV_HEADS * GQA_DIM]
    qg_ref[...] = (qn * GQA_SCALE).astype(BF16)
    if latent:
        cosg, sing = cosg_ref[...], sing_ref[...]
        qgr_ref[...] = (_rope(qn, cosg, sing, GQA_DIM // 4) * GQA_SCALE).astype(BF16)
        k_att = _rope(kn, cosg[:, :GQA_KV_HEADS * GQA_DIM], sing[:, :GQA_KV_HEADS * GQA_DIM], GQA_DIM // 4)
    else:
        sgk_ref[...] = kn
        sgv_ref[...] = d_v
        k_att = kn
    kg_ref[...] = _dot(k_att.astype(BF16), dup_ref[...]).astype(BF16)
    store_t(vgt_ref, d_v)


def _pre(x2d, mods3, mod_row_fn, lw, consts, seq, latent, rope_tabs=None):
    n_tok = x2d.shape[0]
    tb = TOKEN_BLOCK
    n_blk = n_tok // tb
    n_bat = n_tok // seq
    if seq >= tb:
        n_sub, sub_rows = 1, tb
        blk_per_seq = seq // tb
        vt_shape = lambda r: (n_bat, blk_per_seq, r, tb)
        vt_spec = lambda r: pl.BlockSpec((1, 1, r, tb), lambda i: (i // blk_per_seq, i % blk_per_seq, 0, 0))
    else:
        n_sub, sub_rows = tb // seq, seq
        blk_per_seq = 1
        vt_shape = lambda r: (n_bat, 1, r, seq)
        vt_spec = lambda r: pl.BlockSpec((n_sub, 1, r, seq), lambda i: (i, 0, 0, 0))

    def full(a):
        nd = a.ndim
        return pl.BlockSpec(a.shape, lambda i, _n=nd: (0,) * _n)

    tok = lambda w: pl.BlockSpec((tb, w), lambda i: (i, 0))
    tok_shape = lambda w, dt: jax.ShapeDtypeStruct((n_tok, w), dt)

    inputs = [x2d, mods3, lw["gmix"], lw["w_in"], lw["gq"], lw["w_uq"], lw["gkv"], lw["w_ukv"],
              lw["gqn"], lw["gkn"], consts["ones_q"], consts["ones_k"], consts["dup"], consts["csc"]]
    in_specs = [tok(D_MODEL),
                pl.BlockSpec((None, 1, N_MOD * D_MODEL), lambda i: (mod_row_fn(i), 0, 0))]
    in_specs += [full(a) for a in inputs[2:]]
    if latent:
        for t in rope_tabs:
            inputs.append(t)
            in_specs.append(pl.BlockSpec((tb, t.shape[1]), lambda i: (i % blk_per_seq, 0)))

    names = ["qd", "kd", "vdt", "qm", "km", "vmt", "qg", "kg", "vgt", "ucs"]
    km_shape = jax.ShapeDtypeStruct((MLA_HEADS, n_tok, MLA_HEAD_W), BF16)
    km_spec = pl.BlockSpec((MLA_HEADS, tb, MLA_HEAD_W), lambda i: (0, i, 0))
    out_shape = [tok_shape(256, BF16), tok_shape(256, BF16), jax.ShapeDtypeStruct(vt_shape(256), BF16),
                 tok_shape(512, BF16), km_shape, jax.ShapeDtypeStruct(vt_shape(256), BF16),
                 tok_shape(256, BF16), tok_shape(256, BF16), jax.ShapeDtypeStruct(vt_shape(128), BF16),
                 tok_shape(512, BF16)]
    out_specs = [tok(256), tok(256), vt_spec(256), tok(512), km_spec, vt_spec(256),
                 tok(256), tok(256), vt_spec(128), tok(512)]
    if latent:
        names += ["qd_rot", "qm_rot", "qg_rot"]
        out_shape += [tok_shape(256, BF16), tok_shape(512, BF16), tok_shape(256, BF16)]
        out_specs += [tok(256), tok(512), tok(256)]
    else:
        names += ["s_diff_k", "s_diff_v", "s_ckv", "s_kr", "s_gqa_k", "s_gqa_v"]
        widths = [256, 256, MLA_KV_RANK, MLA_ROPE, 128, 128]
        out_shape += [tok_shape(w, F32) for w in widths]
        out_specs += [tok(w) for w in widths]

    outs = pl.pallas_call(
        functools.partial(_pre_kernel, latent, n_sub, sub_rows),
        grid=(n_blk,),
        in_specs=in_specs,
        out_specs=out_specs,
        out_shape=out_shape,
        compiler_params=_params(1),
        name="pre_latent" if latent else "pre_context",
    )(*inputs)
    return dict(zip(names, outs))


def _cache_kernel(dk_ref, dv_ref, ckv_ref, kr_ref, gk_ref, gv_ref, wukv_ref, place_ref, dup_ref,
                  kd_ref, vdt_ref, km_ref, vmt_ref, kg_ref, vgt_ref):
    kd_ref[...] = dk_ref[...].astype(BF16)
    vdt_ref[...] = dv_ref[...].T.astype(BF16)
    kv = _dot(ckv_ref[...].astype(BF16), wukv_ref[...])
    kr4 = _dot(kr_ref[...].astype(BF16), place_ref[...])
    n_k = MLA_HEADS * MLA_HEAD_W
    for hd in range(MLA_HEADS):
        sl = slice(hd * MLA_HEAD_W, (hd + 1) * MLA_HEAD_W)
        km_ref[hd] = (kv[:, sl] + kr4[:, sl]).astype(BF16)
    vmt_ref[...] = kv[:, n_k:].T.astype(BF16)
    kg_ref[...] = _dot(gk_ref[...].astype(BF16), dup_ref[...]).astype(BF16)
    vgt_ref[...] = gv_ref[...].T.astype(BF16)


def _cache_prep(cache_diff_k, cache_diff_v, cache_mla_ckv, cache_mla_krope, cache_gqa_k, cache_gqa_v,
                w_ukv_all, consts):
    db, depth, past = cache_diff_k.shape[:3]
    ins = [cache_diff_k.reshape(db, depth, past, 256), cache_diff_v.reshape(db, depth, past, 256),
           cache_mla_ckv, cache_mla_krope, cache_gqa_k.reshape(db, depth, past, 128),
           cache_gqa_v.reshape(db, depth, past, 128)]
    cspec = lambda w: pl.BlockSpec((None, None, past, w), lambda l, b: (b, l, 0, 0))
    in_specs = [cspec(a.shape[-1]) for a in ins]
    in_specs += [pl.BlockSpec((None,) + w_ukv_all.shape[1:], lambda l, b: (l, 0, 0)),
                 pl.BlockSpec(consts["place_kr"].shape, lambda l, b: (0, 0)),
                 pl.BlockSpec(consts["dup"].shape, lambda l, b: (0, 0))]
    kspec = lambda w: pl.BlockSpec((None, None, past, w), lambda l, b: (l, b, 0, 0))
    vspec = lambda r: pl.BlockSpec((None, None, r, past), lambda l, b: (l, b, 0, 0))
    kshape = lambda w: jax.ShapeDtypeStruct((depth, db, past, w), BF16)
    vshape = lambda r: jax.ShapeDtypeStruct((depth, db, r, past), BF16)
    outs = pl.pallas_call(
        _cache_kernel,
        grid=(depth, db),
        in_specs=in_specs,
        out_specs=[kspec(256), vspec(256),
                   pl.BlockSpec((None, None, MLA_HEADS, past, MLA_HEAD_W), lambda l, b: (l, b, 0, 0, 0)),
                   vspec(256), kspec(256), vspec(128)],
        out_shape=[kshape(256), vshape(256),
                   jax.ShapeDtypeStruct((depth, db, MLA_HEADS, past, MLA_HEAD_W), BF16),
                   vshape(256), kshape(256), vshape(128)],
        compiler_params=_params(2),
        name="cache_prep",
    )(*ins, w_ukv_all, consts["place_kr"], consts["dup"])
    return dict(zip(["kd", "vdt", "km", "vmt", "kg", "vgt"], outs))


def _aligned_ds(start, size):
    if isinstance(start, int):
        return pl.ds(start, size)
    return pl.ds(pl.multiple_of(start, size), size)


def _attn_kernel(group, n_parts, chunk_counts, lam_init, *refs):
    it = iter(refs)
    q_refs = [next(it) for _ in range(n_parts)]
    kv_refs = [(next(it), next(it)) for _ in range(n_parts)]
    if group == "diff":
        lam_ref, g_ref = next(it), next(it)
    o_ref = next(it)
    m_sc, l_sc, acc_sc, qe_sc = next(it), next(it), next(it), next(it)
    s_bufs = (next(it), next(it))
    mc_bufs = (next(it), next(it))

    n_heads = {"diff": 2 * DIFF_HEADS, "mla": MLA_HEADS, "gqa": GQA_HEADS}[group]
    head_shift = n_heads.bit_length() - 1
    tq = o_ref.shape[0]

    m_sc[...] = jnp.full(m_sc.shape, -jnp.inf, F32)
    l_sc[...] = jnp.zeros(l_sc.shape, F32)
    acc_sc[...] = jnp.zeros(acc_sc.shape, F32)
    for part in range(n_parts):
        if group == "mla":
            for hd in range(n_heads):
                qe_sc[part, hd] = q_refs[part][:, hd * MLA_HEAD_W:(hd + 1) * MLA_HEAD_W]
        else:
            q = q_refs[part][...].astype(F32)
            lane = lax.broadcasted_iota(jnp.int32, q.shape, 1)
            head_w = q.shape[1] // n_heads
            for hd in range(n_heads):
                keep = (lane >= hd * head_w) & (lane < (hd + 1) * head_w)
                qe_sc[part, hd] = jnp.where(keep, q, 0.0).astype(BF16)

    def split(item):
        return item >> head_shift, item & (n_heads - 1)

    def scores(part, item, slot):
        k_ref = kv_refs[part][0]
        chunk = chunk_counts[part]
        c, hd = split(item)
        rows = _aligned_ds(c * chunk, chunk)
        k_blk = k_ref[hd, rows, :] if group == "mla" else k_ref[rows, :]
        s = _dot_nt(k_blk, qe_sc[part, hd])
        s_bufs[slot][0:chunk, :] = s
        mc_bufs[slot][...] = jnp.max(s, axis=0, keepdims=True)

    def accumulate(part, item, slot):
        vt_ref = kv_refs[part][1]
        chunk = chunk_counts[part]
        c, hd = split(item)
        v_row = (hd if group == "mla" else hd >> 1) * 64
        vt_blk = vt_ref[c, _aligned_ds(v_row, 64), :]
        m_old = m_sc[hd]
        m_new = jnp.maximum(m_old, mc_bufs[slot][...])
        alpha = jnp.exp(m_old - m_new)
        p = jnp.exp(s_bufs[slot][0:chunk, :] - m_new)
        l_sc[hd] = alpha * l_sc[hd] + jnp.sum(p, axis=0, keepdims=True)
        acc_sc[hd] = alpha * acc_sc[hd] + _dot(vt_blk, p.astype(BF16))
        m_sc[hd] = m_new

    for part in range(n_parts):
        n_items = (kv_refs[part][1].shape[0]) * n_heads
        assert n_items % 2 == 0
        scores(part, 0, 0)
        if n_items > 2:
            def body(j, carry, part=part):
                scores(part, 2 * j + 1, 1)
                accumulate(part, 2 * j, 0)
                scores(part, 2 * j + 2, 0)
                accumulate(part, 2 * j + 1, 1)
                return carry
            lax.fori_loop(0, n_items // 2 - 1, body, 0)
        scores(part, n_items - 1, 1)
        accumulate(part, n_items - 2, 0)
        accumulate(part, n_items - 1, 1)

    outs = [acc_sc[hd] / l_sc[hd] for hd in range(n_heads)]

    if group == "diff":
        lp = lam_ref[...]
        lam = (jnp.exp(jnp.sum(lp[0:1] * lp[1:2], axis=-1, keepdims=True))
               - jnp.exp(jnp.sum(lp[2:3] * lp[3:4], axis=-1, keepdims=True)) + lam_init)
        g = g_ref[...]
        comb = []
        for hd in range(DIFF_HEADS):
            y = outs[2 * hd] - lam * outs[2 * hd + 1]
            ms = jnp.sum(y * y, axis=0, keepdims=True) * (1.0 / DIFF_V)
            comb.append(y * lax.rsqrt(ms + EPS) * g * (1.0 - lam_init))
        outs = comb
    o_ref[...] = jnp.concatenate(outs, axis=0).T.astype(o_ref.dtype)


def _attention(group, q_list, kv_list, n_bat, seq, lam_init=0.0, extras=()):
    tq = min(Q_TILE, seq)
    n_q = seq // tq
    in_specs, inputs, chunk_counts = [], [], []
    for q in q_list:
        inputs.append(q)
        in_specs.append(pl.BlockSpec((tq, q.shape[1]), lambda b, i: (b * n_q + i, 0)))
    for k, vt, kspec, vspec, chunk in kv_list:
        inputs += [k, vt]
        in_specs += [kspec, vspec]
        chunk_counts.append(chunk)
    for e in extras:
        inputs.append(e)
        in_specs.append(pl.BlockSpec(e.shape, lambda b, i, _n=e.ndim: (0,) * _n))
    n_heads = {"diff": 2 * DIFF_HEADS, "mla": MLA_HEADS, "gqa": GQA_HEADS}[group]
    k_width = MLA_HEAD_W if group == "mla" else q_list[0].shape[1]
    max_chunk = max(chunk_counts)
    scratch = [pltpu.VMEM((n_heads, 1, tq), F32), pltpu.VMEM((n_heads, 1, tq), F32),
               pltpu.VMEM((n_heads, 64, tq), F32),
               pltpu.VMEM((len(q_list), n_heads, tq, k_width), BF16),
               pltpu.VMEM((max_chunk, tq), F32), pltpu.VMEM((max_chunk, tq), F32),
               pltpu.VMEM((1, tq), F32), pltpu.VMEM((1, tq), F32)]
    return pl.pallas_call(
        functools.partial(_attn_kernel, group, len(q_list), tuple(chunk_counts), lam_init),
        grid=(n_bat, n_q),
        in_specs=in_specs,
        out_specs=pl.BlockSpec((tq, GROUP_W), lambda b, i: (b * n_q + i, 0)),
        out_shape=jax.ShapeDtypeStruct((n_bat * seq, GROUP_W), BF16),
        scratch_shapes=scratch,
        compiler_params=_params(2),
        name="attn_" + group,
    )(*inputs)


def _own_kv(k, vt, seq):
    if k.ndim == 3:
        kspec = pl.BlockSpec((k.shape[0], seq, k.shape[2]), lambda b, i: (0, b, 0))
    else:
        kspec = pl.BlockSpec((seq, k.shape[1]), lambda b, i: (b, 0))
    vspec = pl.BlockSpec((None,) + vt.shape[1:], lambda b, i: (b, 0, 0, 0))
    return (k, vt, kspec, vspec, vt.shape[3])


def _cache_kv(k, vt, layer):
    past = vt.shape[3]
    vt4 = vt.reshape(vt.shape[0], vt.shape[1], 1, vt.shape[2], past)
    kspec = pl.BlockSpec((None, None) + k.shape[2:], lambda b, i, _n=k.ndim - 2: (layer, b) + (0,) * _n)
    vspec = pl.BlockSpec((None, None, 1, vt.shape[2], past), lambda b, i: (layer, b, 0, 0, 0))
    return (k, vt4, kspec, vspec, past)


def _fourier_kernel(n_bat, cos_ref, sin_ref, ucs_ref, o_ref):
    cs = cos_ref[...]
    sn = sin_ref[...]
    for b in range(n_bat):
        u = ucs_ref[b]
        y = _dot(cs, u[:, :GROUP_W]) + _dot(sn, u[:, GROUP_W:])
        o_ref[b] = y.astype(o_ref.dtype)


def _fourier(ucs, n_bat, seq, cos_tab, sin_tab):
    tm = min(FOURIER_ROWS, seq)
    u3 = ucs.reshape(n_bat, seq, 2 * GROUP_W)
    out = pl.pallas_call(
        functools.partial(_fourier_kernel, n_bat),
        grid=(seq // tm,),
        in_specs=[pl.BlockSpec((tm, seq), lambda i: (i, 0)),
                  pl.BlockSpec((tm, seq), lambda i: (i, 0)),
                  pl.BlockSpec((n_bat, seq, 2 * GROUP_W), lambda i: (0, 0, 0))],
        out_specs=pl.BlockSpec((n_bat, tm, GROUP_W), lambda i: (0, i, 0)),
        out_shape=jax.ShapeDtypeStruct((n_bat, seq, GROUP_W), BF16),
        compiler_params=_params(1),
        name="fourier",
    )(cos_tab, sin_tab, u3)
    return out.reshape(n_bat * seq, GROUP_W)


def _post_kernel(final, x_ref, od_ref, yf_ref, om_ref, og_ref, mod_ref, wout_ref, gmlp_ref,
                 w1_ref, w2_ref, gfin_ref, o_ref):
    mod = mod_ref[...]
    gate1 = mod[:, 2 * D_MODEL:3 * D_MODEL]
    shift2 = mod[:, 3 * D_MODEL:4 * D_MODEL]
    scale2 = mod[:, 4 * D_MODEL:5 * D_MODEL]
    gate2 = mod[:, 5 * D_MODEL:6 * D_MODEL]
    mix = (_dot(od_ref[...], wout_ref[0]) + _dot(yf_ref[...], wout_ref[1])
           + _dot(om_ref[...], wout_ref[2]) + _dot(og_ref[...], wout_ref[3]))
    x1 = x_ref[...] + gate1 * mix
    h = (_rms_full(x1, gmlp_ref[...], D_MODEL) * (1.0 + scale2) + shift2).astype(BF16)
    acc = jnp.zeros(x1.shape, F32)
    for c in range(D_FF // MLP_CHUNK):
        sl = slice(c * MLP_CHUNK, (c + 1) * MLP_CHUNK)
        a = jnp.maximum(_dot(h, w1_ref[:, sl]), 0.0)
        acc = acc + _dot((a * a).astype(BF16), w2_ref[sl, :])
    x2 = x1 + gate2 * acc
    if final:
        x2 = _rms_full(x2, gfin_ref[...], D_MODEL)
    o_ref[...] = x2


def _post(x2d, o_diff, y_f, o_mla, o_gqa, mods3, mod_row_fn, lw, g_final, final, name):
    n_tok = x2d.shape[0]
    tb = TOKEN_BLOCK
    tok = lambda w: pl.BlockSpec((tb, w), lambda i: (i, 0))
    const = lambda a: pl.BlockSpec(a.shape, lambda i, _n=a.ndim: (0,) * _n)
    return pl.pallas_call(
        functools.partial(_post_kernel, final),
        grid=(n_tok // tb,),
        in_specs=[tok(D_MODEL), tok(GROUP_W), tok(GROUP_W), tok(GROUP_W), tok(GROUP_W),
                  pl.BlockSpec((None, 1, N_MOD * D_MODEL), lambda i: (mod_row_fn(i), 0, 0)),
                  const(lw["w_out"]), const(lw["gmlp"]), const(lw["w1"]), const(lw["w2"]), const(g_final)],
        out_specs=tok(D_MODEL),
        out_shape=jax.ShapeDtypeStruct((n_tok, D_MODEL), F32),
        compiler_params=_params(1),
        name=name,
    )(x2d, o_diff, y_f, o_mla, o_gqa, mods3, lw["w_out"], lw["gmlp"], lw["w1"], lw["w2"], g_final)


def _rope_tables(n_lat, head_dim, width, lane0=0):
    a = head_dim // 2
    inv = ROPE_THETA ** (-jnp.arange(0, a, 2, dtype=F32) / a)
    pos = jnp.arange(n_lat, dtype=jnp.int32)
    row = (pos // GRID_W).astype(F32)
    col = (pos % GRID_W).astype(F32)
    lane = jnp.arange(width, dtype=jnp.int32) - lane0
    in_pat = (lane >= 0) & ((lane < head_dim) | (lane0 == 0))
    li = jnp.where(in_pat, lane % head_dim, 0)
    part = li // a
    ii = li % a
    first = (ii // (a // 2)) == 0
    f = ii % (a // 2)
    p = jnp.where(part[None, :] == 0, row[:, None], col[:, None])
    ang = p * inv[f][None, :]
    cos = jnp.where(in_pat[None, :], jnp.cos(ang), 1.0)
    ssin = jnp.where(in_pat[None, :], jnp.where(first[None, :], -jnp.sin(ang), jnp.sin(ang)), 0.0)
    return cos, ssin


def _dft_tables(n):
    j = jnp.arange(n, dtype=jnp.int32)
    m = (j[:, None] * j[None, :]) % n
    ang = m.astype(F32) * (2.0 * math.pi / n)
    s = n ** -0.5
    return (jnp.cos(ang) * s).astype(BF16), (jnp.sin(ang) * s).astype(BF16)


def _constants():
    eye = lambda n: jnp.eye(n, dtype=F32)
    ones_q = jnp.kron(eye(GQA_HEADS), jnp.ones((GQA_DIM, GQA_DIM), F32)).astype(BF16)
    ones_k = jnp.kron(eye(GQA_KV_HEADS), jnp.ones((GQA_DIM, GQA_DIM), F32)).astype(BF16)
    dup = jnp.kron(eye(GQA_KV_HEADS), jnp.concatenate([eye(GQA_DIM), eye(GQA_DIM)], axis=1)).astype(BF16)
    one_head = jnp.concatenate([jnp.zeros((MLA_ROPE, KR_LANE), F32), eye(MLA_ROPE),
                                jnp.zeros((MLA_ROPE, MLA_HEAD_W - KR_LANE - MLA_ROPE), F32)], axis=1)
    place_kr = jnp.tile(one_head, (1, MLA_HEADS)).astype(BF16)
    j = jnp.arange(FNET_CH, dtype=jnp.int32)
    ang = ((j[:, None] * j[None, :]) % FNET_CH).astype(F32) * (2.0 * math.pi / FNET_CH)
    cc = jnp.kron(eye(FNET_GROUPS), jnp.cos(ang) * FNET_CH ** -0.5)
    sc = jnp.kron(eye(FNET_GROUPS), jnp.sin(ang) * FNET_CH ** -0.5)
    csc = jnp.concatenate([cc, -sc], axis=1).astype(BF16)
    return dict(ones_q=ones_q, ones_k=ones_k, dup=dup, place_kr=place_kr, csc=csc)


def _layer_weights(l, norm_mix_g, norm_mlp_g, w_in, mla_q_norm_g, mla_w_uq, mla_kv_norm_g, mla_w_ukv,
                   gqa_q_norm_g, gqa_k_norm_g, w_out, mlp_w1, mlp_w2):
    wi = w_in[l]
    zc = lambda n: jnp.zeros((D_MODEL, n), F32)
    off = [0, 256, 512, 768, 1024, 1216, 1344, 1376, 1632, 1760, 1888]
    seg = [wi[:, off[k]:off[k + 1]] for k in range(10)]
    w_in_p = jnp.concatenate([
        seg[0], seg[1], seg[2], seg[3],
        seg[4], zc(256 - MLA_Q_RANK),
        seg[5],
        zc(KR_LANE), seg[6], zc(MLA_HEAD_W - KR_LANE - MLA_ROPE),
        seg[7], seg[8], seg[9]], axis=1).astype(BF16)
    wuq = mla_w_uq[l].reshape(MLA_Q_RANK, MLA_HEADS, MLA_NOPE + MLA_ROPE)
    wuq = jnp.pad(wuq, ((0, 256 - MLA_Q_RANK), (0, 0), (0, MLA_HEAD_W - MLA_NOPE - MLA_ROPE)))
    w_uq_p = wuq.reshape(256, MLA_HEADS * MLA_HEAD_W).astype(BF16)
    wukv = mla_w_ukv[l].reshape(MLA_KV_RANK, MLA_HEADS, MLA_NOPE + MLA_V)
    wk = jnp.pad(wukv[:, :, :MLA_NOPE], ((0, 0), (0, 0), (0, MLA_HEAD_W - MLA_NOPE)))
    wv = wukv[:, :, MLA_NOPE:]
    w_ukv_p = jnp.concatenate([wk.reshape(MLA_KV_RANK, -1), wv.reshape(MLA_KV_RANK, -1)], axis=1).astype(BF16)
    return dict(
        gmix=norm_mix_g[l][None, :], gmlp=norm_mlp_g[l][None, :], w_in=w_in_p,
        gq=jnp.pad(mla_q_norm_g[l], (0, 256 - MLA_Q_RANK))[None, :], w_uq=w_uq_p,
        gkv=mla_kv_norm_g[l][None, :], w_ukv=w_ukv_p,
        gqn=jnp.tile(gqa_q_norm_g[l], GQA_HEADS)[None, :], gkn=jnp.tile(gqa_k_norm_g[l], GQA_KV_HEADS)[None, :],
        w_out=w_out[l].reshape(4, GROUP_W, D_MODEL).astype(BF16),
        w1=mlp_w1[l].astype(BF16), w2=mlp_w2[l].astype(BF16))


def kernel(x_prompt, x_sample, cache_diff_k, cache_diff_v, cache_mla_ckv, cache_mla_krope, cache_gqa_k, cache_gqa_v, c, c_ctx, norm_mix_g, norm_mlp_g, ada_w, ada_b, w_in, diff_lambda, diff_subln_g, mla_q_norm_g, mla_w_uq, mla_kv_norm_g, mla_w_ukv, gqa_q_norm_g, gqa_k_norm_g, w_out, mlp_w1, mlp_w2, final_norm_g):
    n_ctx_b, seq, _ = x_prompt.shape
    n_lat_b, n_lat, _ = x_sample.shape
    depth = w_in.shape[0]
    assert seq % 128 == 0 and TOKEN_BLOCK % seq == 0 and (n_ctx_b * seq) % TOKEN_BLOCK == 0
    assert n_lat % TOKEN_BLOCK == 0 and n_lat % KEY_CHUNK == 0 and n_lat % GRID_W == 0

    consts = _constants()
    rows = 1 + n_lat_b
    rows_pad = -(-rows // 8) * 8
    cs = jnp.concatenate([c_ctx[None, :], c, jnp.zeros((rows_pad - rows, D_MODEL), F32)], axis=0)
    mods = _adaln(cs, ada_w, ada_b).reshape(depth * rows_pad, 1, N_MOD * D_MODEL)

    rope_tabs = (_rope_tables(n_lat, DIFF_QK, GROUP_W) + _rope_tables(n_lat, MLA_ROPE, MLA_HEAD_W, KR_LANE)
                 + _rope_tables(n_lat, GQA_DIM, GROUP_W))
    dft_ctx = _dft_tables(seq)
    dft_lat = _dft_tables(n_lat)

    layers = [_layer_weights(l, norm_mix_g, norm_mlp_g, w_in, mla_q_norm_g, mla_w_uq, mla_kv_norm_g,
                             mla_w_ukv, gqa_q_norm_g, gqa_k_norm_g, w_out, mlp_w1, mlp_w2)
              for l in range(depth)]
    cache = _cache_prep(cache_diff_k, cache_diff_v, cache_mla_ckv, cache_mla_krope, cache_gqa_k, cache_gqa_v,
                        jnp.stack([lw["w_ukv"] for lw in layers]), consts)

    g_final = final_norm_g[None, :]
    xp = x_prompt.reshape(n_ctx_b * seq, D_MODEL)
    xs = x_sample.reshape(n_lat_b * n_lat, D_MODEL)
    lat_blocks = n_lat // TOKEN_BLOCK
    states = [[] for _ in range(6)]
    for l in range(depth):
        lw = layers[l]
        lam_init = 0.8 - 0.6 * math.exp(-0.3 * l)
        ctx_row = lambda i, _l=l: _l * rows_pad
        lat_row = lambda i, _l=l: _l * rows_pad + 1 + i // lat_blocks
        lam_p = diff_lambda[l].astype(F32)
        final = l == depth - 1

        t = _pre(xp, mods, ctx_row, lw, consts, seq, latent=False)
        g_t = jnp.broadcast_to(diff_subln_g[l][:, None], (DIFF_V, min(Q_TILE, seq)))
        o_a = _attention("diff", [t["qd"]], [_own_kv(t["kd"], t["vdt"], seq)], n_ctx_b, seq, lam_init,
                         extras=(lam_p, g_t))
        o_c = _attention("mla", [t["qm"]], [_own_kv(t["km"], t["vmt"], seq)], n_ctx_b, seq)
        o_d = _attention("gqa", [t["qg"]], [_own_kv(t["kg"], t["vgt"], seq)], n_ctx_b, seq)
        y_f = _fourier(t["ucs"], n_ctx_b, seq, *dft_ctx)
        xp = _post(xp, o_a, y_f, o_c, o_d, mods, ctx_row, lw, g_final, final, "post_context")
        for k, nm in enumerate(["s_diff_k", "s_diff_v", "s_ckv", "s_kr", "s_gqa_k", "s_gqa_v"]):
            states[k].append(t[nm])

        t = _pre(xs, mods, lat_row, lw, consts, n_lat, latent=True, rope_tabs=rope_tabs)
        g_t = jnp.broadcast_to(diff_subln_g[l][:, None], (DIFF_V, min(Q_TILE, n_lat)))
        o_a = _attention("diff", [t["qd_rot"], t["qd"]],
                         [_own_kv(t["kd"], t["vdt"], n_lat), _cache_kv(cache["kd"], cache["vdt"], l)],
                         n_lat_b, n_lat, lam_init, extras=(lam_p, g_t))
        o_c = _attention("mla", [t["qm_rot"], t["qm"]],
                         [_own_kv(t["km"], t["vmt"], n_lat), _cache_kv(cache["km"], cache["vmt"], l)],
                         n_lat_b, n_lat)
        o_d = _attention("gqa", [t["qg_rot"], t["qg"]],
                         [_own_kv(t["kg"], t["vgt"], n_lat), _cache_kv(cache["kg"], cache["vgt"], l)],
                         n_lat_b, n_lat)
        y_f = _fourier(t["ucs"], n_lat_b, n_lat, *dft_lat)
        xs = _post(xs, o_a, y_f, o_c, o_d, mods, lat_row, lw, g_final, final, "post_latent")

    y_prompt = xp.reshape(n_ctx_b, seq, D_MODEL)
    y_sample = xs.reshape(n_lat_b, n_lat, D_MODEL)
    st = [jnp.stack([a.reshape(n_ctx_b, seq, -1) for a in s], axis=1) for s in states]
    new_diff_k = st[0].reshape(n_ctx_b, depth, seq, DIFF_HEADS, 2 * DIFF_QK)
    new_diff_v = st[1].reshape(n_ctx_b, depth, seq, DIFF_HEADS, DIFF_V)
    new_gqa_k = st[4].reshape(n_ctx_b, depth, seq, GQA_KV_HEADS, GQA_DIM)
    new_gqa_v = st[5].reshape(n_ctx_b, depth, seq, GQA_KV_HEADS, GQA_DIM)
    return (y_prompt, y_sample, new_diff_k, new_diff_v, st[2], st[3], new_gqa_k, new_gqa_v)
```

```python
import functools
import math

import jax
import jax.numpy as jnp
from jax import lax
from jax.experimental import pallas as pl
from jax.experimental.pallas import tpu as pltpu

F32 = jnp.float32
BF16 = jnp.bfloat16

D_MODEL = 1024
GRID_W = 64
GROUP_W = 256
DIFF_HEADS = 4
DIFF_QK = 32
DIFF_V = 64
FNET_GROUPS = 4
FNET_CH = 64
MLA_HEADS = 4
MLA_Q_RANK = 192
MLA_KV_RANK = 128
MLA_NOPE = 64
MLA_ROPE = 32
MLA_V = 64
MLA_HEAD_W = 128
GQA_HEADS = 4
GQA_KV_HEADS = 2
GQA_DIM = 64
D_FF = 4 * D_MODEL
N_MOD = 6
ROPE_THETA = 10000.0
EPS = 1e-6
LOG2E = math.log2(math.e)
DIFF_SCALE = DIFF_QK ** -0.5 * LOG2E
MLA_SCALE = (MLA_NOPE + MLA_ROPE) ** -0.5 * LOG2E
GQA_SCALE = GQA_DIM ** -0.5 * LOG2E
ONES_ROWS = 16
PARK_BUFFERS = 8

V7X_VMEM_LIMIT_BYTES = 56 * 1024 * 1024

C_AQ, C_AK, C_AV, C_BU, C_CQ, C_CKV, C_CKR, C_DQ, C_DK, C_DV, IN_W_PAD = (
    0, 256, 512, 768, 1024, 1280, 1408, 1536, 1792, 1920, 2048)
KR_LANE = MLA_NOPE

TOKEN_BLOCK = 512
KEY_CHUNK = 512
Q_TILE = 512
MLP_CHUNK = 1024
ADA_TILE = 1536
FOURIER_ROWS = 256


def _params(n_axes):
    return pltpu.CompilerParams(dimension_semantics=("arbitrary",) * n_axes,
                                vmem_limit_bytes=V7X_VMEM_LIMIT_BYTES)


def _dot(a, b):
    return jnp.dot(a, b, preferred_element_type=F32)


def _dot_nt(a, b):
    return lax.dot_general(a, b, (((1,), (1,)), ((), ())), preferred_element_type=F32)


def _adaln_kernel(c_ref, w_ref, b_ref, o_ref):
    c = c_ref[...]
    s = c / (1.0 + jnp.exp(-c))
    o_ref[...] = _dot(s.astype(BF16), w_ref[...].astype(BF16)) + b_ref[...]


def _adaln(cs, ada_w, ada_b):
    depth = ada_w.shape[0]
    rows = cs.shape[0]
    n_out = ada_w.shape[2]
    return pl.pallas_call(
        _adaln_kernel,
        grid=(depth, n_out // ADA_TILE),
        in_specs=[
            pl.BlockSpec((rows, D_MODEL), lambda l, j: (0, 0)),
            pl.BlockSpec((None, D_MODEL, ADA_TILE), lambda l, j: (l, 0, j)),
            pl.BlockSpec((None, 1, ADA_TILE), lambda l, j: (l, 0, j)),
        ],
        out_specs=pl.BlockSpec((None, rows, ADA_TILE), lambda l, j: (l, 0, j)),
        out_shape=jax.ShapeDtypeStruct((depth, rows, n_out), F32),
        compiler_params=_params(2),
        name="adaln",
    )(cs, ada_w, ada_b.reshape(depth, 1, n_out))


def _rms_full(x, g, n):
    ss = jnp.sum(x * x, axis=-1, keepdims=True)
    return x * lax.rsqrt(ss * (1.0 / n) + EPS) * g


def _group_sumsq(x, ones_bd):
    sq = x * x
    hi = sq.astype(BF16)
    lo = (sq - hi.astype(F32)).astype(BF16)
    return _dot(hi, ones_bd) + _dot(lo, ones_bd)


def _rope(x, cos, ssin, half):
    width = x.shape[-1]
    lane = lax.broadcasted_iota(jnp.int32, x.shape, 1)
    first = (lane & half) == 0
    partner = jnp.where(first, pltpu.roll(x, width - half, 1), pltpu.roll(x, half, 1))
    return x * cos + partner * ssin


def _pre_kernel(latent, n_sub, sub_rows, *refs):
    it = iter(refs)
    x_ref, mod_ref, gmix_ref, win_ref = next(it), next(it), next(it), next(it)
    gq_ref, wuq_ref, gkv_ref, wukv_ref = next(it), next(it), next(it), next(it)
    gqn_ref, gkn_ref, onesq_ref, onesk_ref, dup_ref, csc_ref = (
        next(it), next(it), next(it), next(it), next(it), next(it))
    if latent:
        cosd_ref, sind_ref, cosm_ref, sinm_ref, cosg_ref, sing_ref = (
            next(it), next(it), next(it), next(it), next(it), next(it))
    qd_ref, kd_ref, vdt_ref = next(it), next(it), next(it)
    qm_ref, km_ref, vmt_ref = next(it), next(it), next(it)
    qg_ref, kg_ref, vgt_ref = next(it), next(it), next(it)
    ucs_ref = next(it)
    if latent:
        qdr_ref, qmr_ref, qgr_ref = next(it), next(it), next(it)
    else:
        sdk_ref, sdv_ref, sckv_ref, skr_ref, sgk_ref, sgv_ref = (
            next(it), next(it), next(it), next(it), next(it), next(it))

    x = x_ref[...]
    mod = mod_ref[...]
    shift = mod[:, 0:D_MODEL]
    scale = mod[:, D_MODEL:2 * D_MODEL]
    h = _rms_full(x, gmix_ref[...], D_MODEL) * (1.0 + scale) + shift
    z = _dot(h.astype(BF16), win_ref[...])

    def store_t(ref, val):
        for s in range(n_sub):
            ref[s, 0] = val[s * sub_rows:(s + 1) * sub_rows, :].T.astype(BF16)

    a_q = z[:, C_AQ:C_AQ + GROUP_W]
    a_k = z[:, C_AK:C_AK + GROUP_W]
    a_v = z[:, C_AV:C_AV + GROUP_W]
    qd_ref[...] = (a_q * DIFF_SCALE).astype(BF16)
    if latent:
        cosd, sind = cosd_ref[...], sind_ref[...]
        qdr_ref[...] = (_rope(a_q, cosd, sind, DIFF_QK // 4) * DIFF_SCALE).astype(BF16)
        kd_ref[...] = _rope(a_k, cosd, sind, DIFF_QK // 4).astype(BF16)
    else:
        kd_ref[...] = a_k.astype(BF16)
        sdk_ref[...] = a_k
        sdv_ref[...] = a_v
    store_t(vdt_ref, a_v)

    ucs_ref[...] = _dot(z[:, C_BU:C_BU + GROUP_W].astype(BF16), csc_ref[...]).astype(BF16)

    cqn = _rms_full(z[:, C_CQ:C_CQ + 256], gq_ref[...], MLA_Q_RANK)
    q_m = _dot(cqn.astype(BF16), wuq_ref[...])
    qm_ref[...] = (q_m * MLA_SCALE).astype(BF16)
    ckv = _rms_full(z[:, C_CKV:C_CKV + MLA_KV_RANK], gkv_ref[...], MLA_KV_RANK)
    kv = _dot(ckv.astype(BF16), wukv_ref[...])
    kr = z[:, C_CKR:C_CKR + MLA_HEAD_W]
    if latent:
        cosm, sinm = cosm_ref[...], sinm_ref[...]
        for hd in range(MLA_HEADS):
            sl = slice(hd * MLA_HEAD_W, (hd + 1) * MLA_HEAD_W)
            qmr_ref[:, sl] = (_rope(q_m[:, sl], cosm, sinm, MLA_ROPE // 4) * MLA_SCALE).astype(BF16)
        kr = _rope(kr, cosm, sinm, MLA_ROPE // 4)
    else:
        sckv_ref[...] = ckv
        skr_ref[...] = kr[:, KR_LANE:KR_LANE + MLA_ROPE]
    for hd in range(MLA_HEADS):
        sl = slice(hd * MLA_HEAD_W, (hd + 1) * MLA_HEAD_W)
        km_ref[hd] = (kv[:, sl] + kr).astype(BF16)
    store_t(vmt_ref, kv[:, MLA_HEADS * MLA_HEAD_W:])

    d_q = z[:, C_DQ:C_DQ + GROUP_W]
    qn = d_q * lax.rsqrt(_group_sumsq(d_q, onesq_ref[...]) * (1.0 / GQA_DIM) + EPS) * gqn_ref[...]
    d_k = z[:, C_DK:C_DK + GQA_KV_HEADS * GQA_DIM]
    kn = d_k * lax.rsqrt(_group_sumsq(d_k, onesk_ref[...]) * (1.0 / GQA_DIM) + EPS) * gkn_ref[...]
    d_v = z[:, C_DV:C_DV + GQA_KV_HEADS * GQA_DIM]
    qg_ref[...] = (qn * GQA_SCALE).astype(BF16)
    if latent:
        cosg, sing = cosg_ref[...], sing_ref[...]
        qgr_ref[...] = (_rope(qn, cosg, sing, GQA_DIM // 4) * GQA_SCALE).astype(BF16)
        k_att = _rope(kn, cosg[:, :GQA_KV_HEADS * GQA_DIM], sing[:, :GQA_KV_HEADS * GQA_DIM], GQA_DIM // 4)
    else:
        sgk_ref[...] = kn
        sgv_ref[...] = d_v
        k_att = kn
    kg_ref[...] = _dot(k_att.astype(BF16), dup_ref[...]).astype(BF16)
    store_t(vgt_ref, d_v)


def _pre(x2d, mods3, mod_row_fn, lw, consts, seq, latent, rope_tabs=None):
    n_tok = x2d.shape[0]
    tb = TOKEN_BLOCK
    n_blk = n_tok // tb
    n_bat = n_tok // seq
    if seq >= tb:
        n_sub, sub_rows = 1, tb
        blk_per_seq = seq // tb
        vt_shape = lambda r: (n_bat, blk_per_seq, r, tb)
        vt_spec = lambda r: pl.BlockSpec((1, 1, r, tb), lambda i: (i // blk_per_seq, i % blk_per_seq, 0, 0))
    else:
        n_sub, sub_rows = tb // seq, seq
        blk_per_seq = 1
        vt_shape = lambda r: (n_bat, 1, r, seq)
        vt_spec = lambda r: pl.BlockSpec((n_sub, 1, r, seq), lambda i: (i, 0, 0, 0))

    def full(a):
        nd = a.ndim
        return pl.BlockSpec(a.shape, lambda i, _n=nd: (0,) * _n)

    tok = lambda w: pl.BlockSpec((tb, w), lambda i: (i, 0))
    tok_shape = lambda w, dt: jax.ShapeDtypeStruct((n_tok, w), dt)

    inputs = [x2d, mods3, lw["gmix"], lw["w_in"], lw["gq"], lw["w_uq"], lw["gkv"], lw["w_ukv"],
              lw["gqn"], lw["gkn"], consts["ones_q"], consts["ones_k"], consts["dup"], consts["csc"]]
    in_specs = [tok(D_MODEL),
                pl.BlockSpec((None, 1, N_MOD * D_MODEL), lambda i: (mod_row_fn(i), 0, 0))]
    in_specs += [full(a) for a in inputs[2:]]
    if latent:
        for t in rope_tabs:
            inputs.append(t)
            in_specs.append(pl.BlockSpec((tb, t.shape[1]), lambda i: (i % blk_per_seq, 0)))

    names = ["qd", "kd", "vdt", "qm", "km", "vmt", "qg", "kg", "vgt", "ucs"]
    km_shape = jax.ShapeDtypeStruct((MLA_HEADS, n_tok, MLA_HEAD_W), BF16)
    km_spec = pl.BlockSpec((MLA_HEADS, tb, MLA_HEAD_W), lambda i: (0, i, 0))
    out_shape = [tok_shape(256, BF16), tok_shape(256, BF16), jax.ShapeDtypeStruct(vt_shape(256), BF16),
                 tok_shape(512, BF16), km_shape, jax.ShapeDtypeStruct(vt_shape(256), BF16),
                 tok_shape(256, BF16), tok_shape(256, BF16), jax.ShapeDtypeStruct(vt_shape(128), BF16),
                 tok_shape(512, BF16)]
    out_specs = [tok(256), tok(256), vt_spec(256), tok(512), km_spec, vt_spec(256),
                 tok(256), tok(256), vt_spec(128), tok(512)]
    if latent:
        names += ["qd_rot", "qm_rot", "qg_rot"]
        out_shape += [tok_shape(256, BF16), tok_shape(512, BF16), tok_shape(256, BF16)]
        out_specs += [tok(256), tok(512), tok(256)]
    else:
        names += ["s_diff_k", "s_diff_v", "s_ckv", "s_kr", "s_gqa_k", "s_gqa_v"]
        widths = [256, 256, MLA_KV_RANK, MLA_ROPE, 128, 128]
        out_shape += [tok_shape(w, F32) for w in widths]
        out_specs += [tok(w) for w in widths]

    outs = pl.pallas_call(
        functools.partial(_pre_kernel, latent, n_sub, sub_rows),
        grid=(n_blk,),
        in_specs=in_specs,
        out_specs=out_specs,
        out_shape=out_shape,
        compiler_params=_params(1),
        name="pre_latent" if latent else "pre_context",
    )(*inputs)
    return dict(zip(names, outs))


def _cache_kernel(dk_ref, dv_ref, ckv_ref, kr_ref, gk_ref, gv_ref, wukv_ref, place_ref, dup_ref,
                  kd_ref, vdt_ref, km_ref, vmt_ref, kg_ref, vgt_ref):
    kd_ref[...] = dk_ref[...].astype(BF16)
    vdt_ref[...] = dv_ref[...].T.astype(BF16)
    kv = _dot(ckv_ref[...].astype(BF16), wukv_ref[...])
    kr4 = _dot(kr_ref[...].astype(BF16), place_ref[...])
    n_k = MLA_HEADS * MLA_HEAD_W
    for hd in range(MLA_HEADS):
        sl = slice(hd * MLA_HEAD_W, (hd + 1) * MLA_HEAD_W)
        km_ref[hd] = (kv[:, sl] + kr4[:, sl]).astype(BF16)
    vmt_ref[...] = kv[:, n_k:].T.astype(BF16)
    kg_ref[...] = _dot(gk_ref[...].astype(BF16), dup_ref[...]).astype(BF16)
    vgt_ref[...] = gv_ref[...].T.astype(BF16)


def _cache_prep(cache_diff_k, cache_diff_v, cache_mla_ckv, cache_mla_krope, cache_gqa_k, cache_gqa_v,
                w_ukv_all, consts):
    db, depth, past = cache_diff_k.shape[:3]
    ins = [cache_diff_k.reshape(db, depth, past, 256), cache_diff_v.reshape(db, depth, past, 256),
           cache_mla_ckv, cache_mla_krope, cache_gqa_k.reshape(db, depth, past, 128),
           cache_gqa_v.reshape(db, depth, past, 128)]
    cspec = lambda w: pl.BlockSpec((None, None, past, w), lambda l, b: (b, l, 0, 0))
    in_specs = [cspec(a.shape[-1]) for a in ins]
    in_specs += [pl.BlockSpec((None,) + w_ukv_all.shape[1:], lambda l, b: (l, 0, 0)),
                 pl.BlockSpec(consts["place_kr"].shape, lambda l, b: (0, 0)),
                 pl.BlockSpec(consts["dup"].shape, lambda l, b: (0, 0))]
    kspec = lambda w: pl.BlockSpec((None, None, past, w), lambda l, b: (l, b, 0, 0))
    vspec = lambda r: pl.BlockSpec((None, None, r, past), lambda l, b: (l, b, 0, 0))
    kshape = lambda w: jax.ShapeDtypeStruct((depth, db, past, w), BF16)
    vshape = lambda r: jax.ShapeDtypeStruct((depth, db, r, past), BF16)
    outs = pl.pallas_call(
        _cache_kernel,
        grid=(depth, db),
        in_specs=in_specs,
        out_specs=[kspec(256), vspec(256),
                   pl.BlockSpec((None, None, MLA_HEADS, past, MLA_HEAD_W), lambda l, b: (l, b, 0, 0, 0)),
                   vspec(256), kspec(256), vspec(128)],
        out_shape=[kshape(256), vshape(256),
                   jax.ShapeDtypeStruct((depth, db, MLA_HEADS, past, MLA_HEAD_W), BF16),
                   vshape(256), kshape(256), vshape(128)],
        compiler_params=_params(2),
        name="cache_prep",
    )(*ins, w_ukv_all, consts["place_kr"], consts["dup"])
    return dict(zip(["kd", "vdt", "km", "vmt", "kg", "vgt"], outs))


def _aligned_ds(start, size):
    if isinstance(start, int):
        return pl.ds(start, size)
    return pl.ds(pl.multiple_of(start, size), size)


def _attn_kernel(group, n_parts, chunk_counts, lam_init, *refs):
    it = iter(refs)
    q_refs = [next(it) for _ in range(n_parts)]
    kv_refs = [(next(it), next(it)) for _ in range(n_parts)]
    if group == "diff":
        lam_ref, g_ref = next(it), next(it)
    o_ref = next(it)
    m_sc, acc_sc, qe_sc = next(it), next(it), next(it)
    s_bufs = [next(it) for _ in range(PARK_BUFFERS)]
    mc_bufs = [next(it) for _ in range(PARK_BUFFERS)]

    n_heads = {"diff": 2 * DIFF_HEADS, "mla": MLA_HEADS, "gqa": GQA_HEADS}[group]
    head_shift = n_heads.bit_length() - 1
    tq = o_ref.shape[0]

    m_sc[...] = jnp.full(m_sc.shape, -jnp.inf, F32)
    acc_sc[...] = jnp.zeros(acc_sc.shape, F32)
    for part in range(n_parts):
        if group == "mla":
            for hd in range(n_heads):
                qe_sc[part, hd] = q_refs[part][:, hd * MLA_HEAD_W:(hd + 1) * MLA_HEAD_W]
        else:
            q = q_refs[part][...].astype(F32)
            lane = lax.broadcasted_iota(jnp.int32, q.shape, 1)
            head_w = q.shape[1] // n_heads
            for hd in range(n_heads):
                keep = (lane >= hd * head_w) & (lane < (hd + 1) * head_w)
                qe_sc[part, hd] = jnp.where(keep, q, 0.0).astype(BF16)

    def split(item):
        return item >> head_shift, item & (n_heads - 1)

    def scores(part, item, slot):
        k_ref = kv_refs[part][0]
        chunk = chunk_counts[part]
        c, hd = split(item)
        rows = _aligned_ds(c * chunk, chunk)
        k_blk = k_ref[hd, rows, :] if group == "mla" else k_ref[rows, :]
        s = _dot_nt(k_blk, qe_sc[part, hd])
        s_bufs[slot][0:chunk, :] = s
        mc_bufs[slot][...] = jnp.max(s, axis=0, keepdims=True)

    def accumulate(part, item, slot):
        vt_ref = kv_refs[part][1]
        chunk = chunk_counts[part]
        c, hd = split(item)
        v_row = (hd if group == "mla" else hd >> 1) * 64
        vt_blk = vt_ref[c, _aligned_ds(v_row, 64), :]
        lhs = jnp.concatenate([vt_blk, jnp.ones((ONES_ROWS, chunk), BF16)], axis=0)
        m_old = m_sc[hd]
        m_new = jnp.maximum(m_old, mc_bufs[slot][...])
        alpha = jnp.exp2(m_old - m_new)
        p = jnp.exp2(s_bufs[slot][0:chunk, :] - m_new)
        acc_sc[hd] = alpha * acc_sc[hd] + _dot(lhs, p.astype(BF16))
        m_sc[hd] = m_new

    grp = PARK_BUFFERS // 2

    def group_scores(part, g, bset):
        for k in range(grp):
            scores(part, grp * g + k, grp * bset + k)

    def group_accumulate(part, g, bset):
        for k in range(grp):
            accumulate(part, grp * g + k, grp * bset + k)

    n_groups = [kv_refs[part][1].shape[0] * n_heads // grp for part in range(n_parts)]
    assert all(n * grp == kv_refs[part][1].shape[0] * n_heads for part, n in enumerate(n_groups))
    last = n_parts - 1
    inline = [(part, g) for part in range(last) for g in range(n_groups[part])]
    off = len(inline) % 2
    group_scores(*(inline[0] if inline else (last, 0)), 0)
    for idx, (part, g) in enumerate(inline):
        nxt = inline[idx + 1] if idx + 1 < len(inline) else (last, 0)
        group_scores(*nxt, (idx + 1) % 2)
        group_accumulate(part, g, idx % 2)

    def two_steps(j, carry):
        group_scores(last, 2 * j + 1, 1 - off)
        group_accumulate(last, 2 * j, off)
        group_scores(last, 2 * j + 2, off)
        group_accumulate(last, 2 * j + 1, 1 - off)
        return carry

    n_loop = (n_groups[last] - 1) // 2
    if n_loop > 1:
        lax.fori_loop(0, n_loop, two_steps, 0)
    else:
        for j in range(n_loop):
            two_steps(j, 0)
    for g in range(2 * n_loop, n_groups[last]):
        if g + 1 < n_groups[last]:
            group_scores(last, g + 1, (off + g + 1) % 2)
        group_accumulate(last, g, (off + g) % 2)

    outs = [acc_sc[hd, 0:64, :] / acc_sc[hd, 64:65, :] for hd in range(n_heads)]

    if group == "diff":
        lp = lam_ref[...]
        lam = (jnp.exp(jnp.sum(lp[0:1] * lp[1:2], axis=-1, keepdims=True))
               - jnp.exp(jnp.sum(lp[2:3] * lp[3:4], axis=-1, keepdims=True)) + lam_init)
        g = g_ref[...]
        comb = []
        for hd in range(DIFF_HEADS):
            y = outs[2 * hd] - lam * outs[2 * hd + 1]
            ms = jnp.sum(y * y, axis=0, keepdims=True) * (1.0 / DIFF_V)
            comb.append(y * lax.rsqrt(ms + EPS) * g * (1.0 - lam_init))
        outs = comb
    o_ref[...] = jnp.concatenate(outs, axis=0).T.astype(o_ref.dtype)


def _attention(group, q_list, kv_list, n_bat, seq, lam_init=0.0, extras=()):
    tq = min(Q_TILE, seq)
    n_q = seq // tq
    in_specs, inputs, chunk_counts = [], [], []
    for q in q_list:
        inputs.append(q)
        in_specs.append(pl.BlockSpec((tq, q.shape[1]), lambda b, i: (b * n_q + i, 0)))
    for k, vt, kspec, vspec, chunk in kv_list:
        inputs += [k, vt]
        in_specs += [kspec, vspec]
        chunk_counts.append(chunk)
    for e in extras:
        inputs.append(e)
        in_specs.append(pl.BlockSpec(e.shape, lambda b, i, _n=e.ndim: (0,) * _n))
    n_heads = {"diff": 2 * DIFF_HEADS, "mla": MLA_HEADS, "gqa": GQA_HEADS}[group]
    k_width = MLA_HEAD_W if group == "mla" else q_list[0].shape[1]
    max_chunk = max(chunk_counts)
    scratch = [pltpu.VMEM((n_heads, 1, tq), F32),
               pltpu.VMEM((n_heads, 64 + ONES_ROWS, tq), F32),
               pltpu.VMEM((len(q_list), n_heads, tq, k_width), BF16)]
    scratch += [pltpu.VMEM((max_chunk, tq), F32)] * PARK_BUFFERS
    scratch += [pltpu.VMEM((1, tq), F32)] * PARK_BUFFERS
    return pl.pallas_call(
        functools.partial(_attn_kernel, group, len(q_list), tuple(chunk_counts), lam_init),
        grid=(n_bat, n_q),
        in_specs=in_specs,
        out_specs=pl.BlockSpec((tq, GROUP_W), lambda b, i: (b * n_q + i, 0)),
        out_shape=jax.ShapeDtypeStruct((n_bat * seq, GROUP_W), BF16),
        scratch_shapes=scratch,
        compiler_params=_params(2),
        name="attn_" + group,
    )(*inputs)


def _own_kv(k, vt, seq):
    if k.ndim == 3:
        kspec = pl.BlockSpec((k.shape[0], seq, k.shape[2]), lambda b, i: (0, b, 0))
    else:
        kspec = pl.BlockSpec((seq, k.shape[1]), lambda b, i: (b, 0))
    vspec = pl.BlockSpec((None,) + vt.shape[1:], lambda b, i: (b, 0, 0, 0))
    return (k, vt, kspec, vspec, vt.shape[3])


def _cache_kv(k, vt, layer):
    past = vt.shape[3]
    vt4 = vt.reshape(vt.shape[0], vt.shape[1], 1, vt.shape[2], past)
    kspec = pl.BlockSpec((None, None) + k.shape[2:], lambda b, i, _n=k.ndim - 2: (layer, b) + (0,) * _n)
    vspec = pl.BlockSpec((None, None, 1, vt.shape[2], past), lambda b, i: (layer, b, 0, 0, 0))
    return (k, vt4, kspec, vspec, past)


def _fourier_kernel(n_bat, cos_ref, sin_ref, ucs_ref, o_ref):
    cs = cos_ref[...]
    sn = sin_ref[...]
    for b in range(n_bat):
        u = ucs_ref[b]
        y = _dot(cs, u[:, :GROUP_W]) + _dot(sn, u[:, GROUP_W:])
        o_ref[b] = y.astype(o_ref.dtype)


def _fourier(ucs, n_bat, seq, cos_tab, sin_tab):
    tm = min(FOURIER_ROWS, seq)
    u3 = ucs.reshape(n_bat, seq, 2 * GROUP_W)
    out = pl.pallas_call(
        functools.partial(_fourier_kernel, n_bat),
        grid=(seq // tm,),
        in_specs=[pl.BlockSpec((tm, seq), lambda i: (i, 0)),
                  pl.BlockSpec((tm, seq), lambda i: (i, 0)),
                  pl.BlockSpec((n_bat, seq, 2 * GROUP_W), lambda i: (0, 0, 0))],
        out_specs=pl.BlockSpec((n_bat, tm, GROUP_W), lambda i: (0, i, 0)),
        out_shape=jax.ShapeDtypeStruct((n_bat, seq, GROUP_W), BF16),
        compiler_params=_params(1),
        name="fourier",
    )(cos_tab, sin_tab, u3)
    return out.reshape(n_bat * seq, GROUP_W)


def _post_kernel(final, x_ref, od_ref, yf_ref, om_ref, og_ref, mod_ref, wout_ref, gmlp_ref,
                 w1_ref, w2_ref, gfin_ref, o_ref):
    mod = mod_ref[...]
    gate1 = mod[:, 2 * D_MODEL:3 * D_MODEL]
    shift2 = mod[:, 3 * D_MODEL:4 * D_MODEL]
    scale2 = mod[:, 4 * D_MODEL:5 * D_MODEL]
    gate2 = mod[:, 5 * D_MODEL:6 * D_MODEL]
    mix = (_dot(od_ref[...], wout_ref[0]) + _dot(yf_ref[...], wout_ref[1])
           + _dot(om_ref[...], wout_ref[2]) + _dot(og_ref[...], wout_ref[3]))
    x1 = x_ref[...] + gate1 * mix
    h = (_rms_full(x1, gmlp_ref[...], D_MODEL) * (1.0 + scale2) + shift2).astype(BF16)
    acc = jnp.zeros(x1.shape, F32)
    for c in range(D_FF // MLP_CHUNK):
        sl = slice(c * MLP_CHUNK, (c + 1) * MLP_CHUNK)
        a = jnp.maximum(_dot(h, w1_ref[:, sl]), 0.0)
        acc = acc + _dot((a * a).astype(BF16), w2_ref[sl, :])
    x2 = x1 + gate2 * acc
    if final:
        x2 = _rms_full(x2, gfin_ref[...], D_MODEL)
    o_ref[...] = x2


def _post(x2d, o_diff, y_f, o_mla, o_gqa, mods3, mod_row_fn, lw, g_final, final, name):
    n_tok = x2d.shape[0]
    tb = TOKEN_BLOCK
    tok = lambda w: pl.BlockSpec((tb, w), lambda i: (i, 0))
    const = lambda a: pl.BlockSpec(a.shape, lambda i, _n=a.ndim: (0,) * _n)
    return pl.pallas_call(
        functools.partial(_post_kernel, final),
        grid=(n_tok // tb,),
        in_specs=[tok(D_MODEL), tok(GROUP_W), tok(GROUP_W), tok(GROUP_W), tok(GROUP_W),
                  pl.BlockSpec((None, 1, N_MOD * D_MODEL), lambda i: (mod_row_fn(i), 0, 0)),
                  const(lw["w_out"]), const(lw["gmlp"]), const(lw["w1"]), const(lw["w2"]), const(g_final)],
        out_specs=tok(D_MODEL),
        out_shape=jax.ShapeDtypeStruct((n_tok, D_MODEL), F32),
        compiler_params=_params(1),
        name=name,
    )(x2d, o_diff, y_f, o_mla, o_gqa, mods3, lw["w_out"], lw["gmlp"], lw["w1"], lw["w2"], g_final)


def _rope_tables(n_lat, head_dim, width, lane0=0):
    a = head_dim // 2
    inv = ROPE_THETA ** (-jnp.arange(0, a, 2, dtype=F32) / a)
    pos = jnp.arange(n_lat, dtype=jnp.int32)
    row = (pos // GRID_W).astype(F32)
    col = (pos % GRID_W).astype(F32)
    lane = jnp.arange(width, dtype=jnp.int32) - lane0
    in_pat = (lane >= 0) & ((lane < head_dim) | (lane0 == 0))
    li = jnp.where(in_pat, lane % head_dim, 0)
    part = li // a
    ii = li % a
    first = (ii // (a // 2)) == 0
    f = ii % (a // 2)
    p = jnp.where(part[None, :] == 0, row[:, None], col[:, None])
    ang = p * inv[f][None, :]
    cos = jnp.where(in_pat[None, :], jnp.cos(ang), 1.0)
    ssin = jnp.where(in_pat[None, :], jnp.where(first[None, :], -jnp.sin(ang), jnp.sin(ang)), 0.0)
    return cos, ssin


def _dft_tables(n):
    j = jnp.arange(n, dtype=jnp.int32)
    m = (j[:, None] * j[None, :]) % n
    ang = m.astype(F32) * (2.0 * math.pi / n)
    s = n ** -0.5
    return (jnp.cos(ang) * s).astype(BF16), (jnp.sin(ang) * s).astype(BF16)


def _constants():
    eye = lambda n: jnp.eye(n, dtype=F32)
    ones_q = jnp.kron(eye(GQA_HEADS), jnp.ones((GQA_DIM, GQA_DIM), F32)).astype(BF16)
    ones_k = jnp.kron(eye(GQA_KV_HEADS), jnp.ones((GQA_DIM, GQA_DIM), F32)).astype(BF16)
    dup = jnp.kron(eye(GQA_KV_HEADS), jnp.concatenate([eye(GQA_DIM), eye(GQA_DIM)], axis=1)).astype(BF16)
    one_head = jnp.concatenate([jnp.zeros((MLA_ROPE, KR_LANE), F32), eye(MLA_ROPE),
                                jnp.zeros((MLA_ROPE, MLA_HEAD_W - KR_LANE - MLA_ROPE), F32)], axis=1)
    place_kr = jnp.tile(one_head, (1, MLA_HEADS)).astype(BF16)
    j = jnp.arange(FNET_CH, dtype=jnp.int32)
    ang = ((j[:, None] * j[None, :]) % FNET_CH).astype(F32) * (2.0 * math.pi / FNET_CH)
    cc = jnp.kron(eye(FNET_GROUPS), jnp.cos(ang) * FNET_CH ** -0.5)
    sc = jnp.kron(eye(FNET_GROUPS), jnp.sin(ang) * FNET_CH ** -0.5)
    csc = jnp.concatenate([cc, -sc], axis=1).astype(BF16)
    return dict(ones_q=ones_q, ones_k=ones_k, dup=dup, place_kr=place_kr, csc=csc)


def _layer_weights(l, norm_mix_g, norm_mlp_g, w_in, mla_q_norm_g, mla_w_uq, mla_kv_norm_g, mla_w_ukv,
                   gqa_q_norm_g, gqa_k_norm_g, w_out, mlp_w1, mlp_w2):
    wi = w_in[l]
    zc = lambda n: jnp.zeros((D_MODEL, n), F32)
    off = [0, 256, 512, 768, 1024, 1216, 1344, 1376, 1632, 1760, 1888]
    seg = [wi[:, off[k]:off[k + 1]] for k in range(10)]
    w_in_p = jnp.concatenate([
        seg[0], seg[1], seg[2], seg[3],
        seg[4], zc(256 - MLA_Q_RANK),
        seg[5],
        zc(KR_LANE), seg[6], zc(MLA_HEAD_W - KR_LANE - MLA_ROPE),
        seg[7], seg[8], seg[9]], axis=1).astype(BF16)
    wuq = mla_w_uq[l].reshape(MLA_Q_RANK, MLA_HEADS, MLA_NOPE + MLA_ROPE)
    wuq = jnp.pad(wuq, ((0, 256 - MLA_Q_RANK), (0, 0), (0, MLA_HEAD_W - MLA_NOPE - MLA_ROPE)))
    w_uq_p = wuq.reshape(256, MLA_HEADS * MLA_HEAD_W).astype(BF16)
    wukv = mla_w_ukv[l].reshape(MLA_KV_RANK, MLA_HEADS, MLA_NOPE + MLA_V)
    wk = jnp.pad(wukv[:, :, :MLA_NOPE], ((0, 0), (0, 0), (0, MLA_HEAD_W - MLA_NOPE)))
    wv = wukv[:, :, MLA_NOPE:]
    w_ukv_p = jnp.concatenate([wk.reshape(MLA_KV_RANK, -1), wv.reshape(MLA_KV_RANK, -1)], axis=1).astype(BF16)
    return dict(
        gmix=norm_mix_g[l][None, :], gmlp=norm_mlp_g[l][None, :], w_in=w_in_p,
        gq=jnp.pad(mla_q_norm_g[l], (0, 256 - MLA_Q_RANK))[None, :], w_uq=w_uq_p,
        gkv=mla_kv_norm_g[l][None, :], w_ukv=w_ukv_p,
        gqn=jnp.tile(gqa_q_norm_g[l], GQA_HEADS)[None, :], gkn=jnp.tile(gqa_k_norm_g[l], GQA_KV_HEADS)[None, :],
        w_out=w_out[l].reshape(4, GROUP_W, D_MODEL).astype(BF16),
        w1=mlp_w1[l].astype(BF16), w2=mlp_w2[l].astype(BF16))


def kernel(x_prompt, x_sample, cache_diff_k, cache_diff_v, cache_mla_ckv, cache_mla_krope, cache_gqa_k, cache_gqa_v, c, c_ctx, norm_mix_g, norm_mlp_g, ada_w, ada_b, w_in, diff_lambda, diff_subln_g, mla_q_norm_g, mla_w_uq, mla_kv_norm_g, mla_w_ukv, gqa_q_norm_g, gqa_k_norm_g, w_out, mlp_w1, mlp_w2, final_norm_g):
    n_ctx_b, seq, _ = x_prompt.shape
    n_lat_b, n_lat, _ = x_sample.shape
    depth = w_in.shape[0]
    assert seq % 128 == 0 and TOKEN_BLOCK % seq == 0 and (n_ctx_b * seq) % TOKEN_BLOCK == 0
    assert n_lat % TOKEN_BLOCK == 0 and n_lat % KEY_CHUNK == 0 and n_lat % GRID_W == 0

    consts = _constants()
    rows = 1 + n_lat_b
    rows_pad = -(-rows // 8) * 8
    cs = jnp.concatenate([c_ctx[None, :], c, jnp.zeros((rows_pad - rows, D_MODEL), F32)], axis=0)
    mods = _adaln(cs, ada_w, ada_b).reshape(depth * rows_pad, 1, N_MOD * D_MODEL)

    rope_tabs = (_rope_tables(n_lat, DIFF_QK, GROUP_W) + _rope_tables(n_lat, MLA_ROPE, MLA_HEAD_W, KR_LANE)
                 + _rope_tables(n_lat, GQA_DIM, GROUP_W))
    dft_ctx = _dft_tables(seq)
    dft_lat = _dft_tables(n_lat)

    layers = [_layer_weights(l, norm_mix_g, norm_mlp_g, w_in, mla_q_norm_g, mla_w_uq, mla_kv_norm_g,
                             mla_w_ukv, gqa_q_norm_g, gqa_k_norm_g, w_out, mlp_w1, mlp_w2)
              for l in range(depth)]
    cache = _cache_prep(cache_diff_k, cache_diff_v, cache_mla_ckv, cache_mla_krope, cache_gqa_k, cache_gqa_v,
                        jnp.stack([lw["w_ukv"] for lw in layers]), consts)

    g_final = final_norm_g[None, :]
    xp = x_prompt.reshape(n_ctx_b * seq, D_MODEL)
    xs = x_sample.reshape(n_lat_b * n_lat, D_MODEL)
    lat_blocks = n_lat // TOKEN_BLOCK
    states = [[] for _ in range(6)]
    for l in range(depth):
        lw = layers[l]
        lam_init = 0.8 - 0.6 * math.exp(-0.3 * l)
        ctx_row = lambda i, _l=l: _l * rows_pad
        lat_row = lambda i, _l=l: _l * rows_pad + 1 + i // lat_blocks
        lam_p = diff_lambda[l].astype(F32)
        final = l == depth - 1

        t = _pre(xp, mods, ctx_row, lw, consts, seq, latent=False)
        g_t = jnp.broadcast_to(diff_subln_g[l][:, None], (DIFF_V, min(Q_TILE, seq)))
        o_a = _attention("diff", [t["qd"]], [_own_kv(t["kd"], t["vdt"], seq)], n_ctx_b, seq, lam_init,
                         extras=(lam_p, g_t))
        o_c = _attention("mla", [t["qm"]], [_own_kv(t["km"], t["vmt"], seq)], n_ctx_b, seq)
        o_d = _attention("gqa", [t["qg"]], [_own_kv(t["kg"], t["vgt"], seq)], n_ctx_b, seq)
        y_f = _fourier(t["ucs"], n_ctx_b, seq, *dft_ctx)
        xp = _post(xp, o_a, y_f, o_c, o_d, mods, ctx_row, lw, g_final, final, "post_context")
        for k, nm in enumerate(["s_diff_k", "s_diff_v", "s_ckv", "s_kr", "s_gqa_k", "s_gqa_v"]):
            states[k].append(t[nm])

        t = _pre(xs, mods, lat_row, lw, consts, n_lat, latent=True, rope_tabs=rope_tabs)
        g_t = jnp.broadcast_to(diff_subln_g[l][:, None], (DIFF_V, min(Q_TILE, n_lat)))
        o_a = _attention("diff", [t["qd"], t["qd_rot"]],
                         [_cache_kv(cache["kd"], cache["vdt"], l), _own_kv(t["kd"], t["vdt"], n_lat)],
                         n_lat_b, n_lat, lam_init, extras=(lam_p, g_t))
        o_c = _attention("mla", [t["qm"], t["qm_rot"]],
                         [_cache_kv(cache["km"], cache["vmt"], l), _own_kv(t["km"], t["vmt"], n_lat)],
                         n_lat_b, n_lat)
        o_d = _attention("gqa", [t["qg"], t["qg_rot"]],
                         [_cache_kv(cache["kg"], cache["vgt"], l), _own_kv(t["kg"], t["vgt"], n_lat)],
                         n_lat_b, n_lat)
        y_f = _fourier(t["ucs"], n_lat_b, n_lat, *dft_lat)
        xs = _post(xs, o_a, y_f, o_c, o_d, mods, lat_row, lw, g_final, final, "post_latent")

    y_prompt = xp.reshape(n_ctx_b, seq, D_MODEL)
    y_sample = xs.reshape(n_lat_b, n_lat, D_MODEL)
    st = [jnp.stack([a.reshape(n_ctx_b, seq, -1) for a in s], axis=1) for s in states]
    new_diff_k = st[0].reshape(n_ctx_b, depth, seq, DIFF_HEADS, 2 * DIFF_QK)
    new_diff_v = st[1].reshape(n_ctx_b, depth, seq, DIFF_HEADS, DIFF_V)
    new_gqa_k = st[4].reshape(n_ctx_b, depth, seq, GQA_KV_HEADS, GQA_DIM)
    new_gqa_v = st[5].reshape(n_ctx_b, depth, seq, GQA_KV_HEADS, GQA_DIM)
    return (y_prompt, y_sample, new_diff_k, new_diff_v, st[2], st[3], new_gqa_k, new_gqa_v)
```

```python
import functools
import math

import jax
import jax.numpy as jnp
from jax import lax
from jax.experimental import pallas as pl
from jax.experimental.pallas import tpu as pltpu

F32 = jnp.float32
BF16 = jnp.bfloat16

D_MODEL = 1024
GRID_W = 64
GROUP_W = 256
DIFF_HEADS = 4
DIFF_QK = 32
DIFF_V = 64
FNET_GROUPS = 4
FNET_CH = 64
MLA_HEADS = 4
MLA_Q_RANK = 192
MLA_KV_RANK = 128
MLA_NOPE = 64
MLA_ROPE = 32
MLA_V = 64
MLA_HEAD_W = 128
GQA_HEADS = 4
GQA_KV_HEADS = 2
GQA_DIM = 64
D_FF = 4 * D_MODEL
N_MOD = 6
ROPE_THETA = 10000.0
EPS = 1e-6
LOG2E = math.log2(math.e)
DIFF_SCALE = DIFF_QK ** -0.5 * LOG2E
MLA_SCALE = (MLA_NOPE + MLA_ROPE) ** -0.5 * LOG2E
GQA_SCALE = GQA_DIM ** -0.5 * LOG2E
ONES_ROWS = 16
PARK_BUFFERS = 8

V7X_VMEM_LIMIT_BYTES = 56 * 1024 * 1024

C_AQ, C_AK, C_AV, C_BU, C_CQ, C_CKV, C_CKR, C_DQ, C_DK, C_DV, IN_W_PAD = (
    0, 256, 512, 768, 1024, 1280, 1408, 1536, 1792, 1920, 2048)
KR_LANE = MLA_NOPE

TOKEN_BLOCK = 512
KEY_CHUNK = 512
Q_TILE = 512
MLP_CHUNK = 1024
ADA_TILE = 1536
FOURIER_ROWS = 256


def _params(n_axes):
    return pltpu.CompilerParams(dimension_semantics=("arbitrary",) * n_axes,
                                vmem_limit_bytes=V7X_VMEM_LIMIT_BYTES)


def _dot(a, b):
    return jnp.dot(a, b, preferred_element_type=F32)


def _dot_nt(a, b):
    return lax.dot_general(a, b, (((1,), (1,)), ((), ())), preferred_element_type=F32)


def _adaln_kernel(c_ref, w_ref, b_ref, o_ref):
    c = c_ref[...]
    s = c / (1.0 + jnp.exp(-c))
    o_ref[...] = _dot(s.astype(BF16), w_ref[...].astype(BF16)) + b_ref[...]


def _adaln(cs, ada_w, ada_b):
    depth = ada_w.shape[0]
    rows = cs.shape[0]
    n_out = ada_w.shape[2]
    return pl.pallas_call(
        _adaln_kernel,
        grid=(depth, n_out // ADA_TILE),
        in_specs=[
            pl.BlockSpec((rows, D_MODEL), lambda l, j: (0, 0)),
            pl.BlockSpec((None, D_MODEL, ADA_TILE), lambda l, j: (l, 0, j)),
            pl.BlockSpec((None, 1, ADA_TILE), lambda l, j: (l, 0, j)),
        ],
        out_specs=pl.BlockSpec((None, rows, ADA_TILE), lambda l, j: (l, 0, j)),
        out_shape=jax.ShapeDtypeStruct((depth, rows, n_out), F32),
        compiler_params=_params(2),
        name="adaln",
    )(cs, ada_w, ada_b.reshape(depth, 1, n_out))


def _rms_full(x, g, n):
    ss = jnp.sum(x * x, axis=-1, keepdims=True)
    return x * lax.rsqrt(ss * (1.0 / n) + EPS) * g


def _group_sumsq(x, ones_bd):
    sq = x * x
    hi = sq.astype(BF16)
    lo = (sq - hi.astype(F32)).astype(BF16)
    return _dot(hi, ones_bd) + _dot(lo, ones_bd)


def _rope(x, cos, ssin, half):
    width = x.shape[-1]
    lane = lax.broadcasted_iota(jnp.int32, x.shape, 1)
    first = (lane & half) == 0
    partner = jnp.where(first, pltpu.roll(x, width - half, 1), pltpu.roll(x, half, 1))
    return x * cos + partner * ssin


def _pre_kernel(latent, n_sub, sub_rows, *refs):
    it = iter(refs)
    x_ref, mod_ref, gmix_ref, win_ref = next(it), next(it), next(it), next(it)
    gq_ref, wuq_ref, gkv_ref, wukv_ref = next(it), next(it), next(it), next(it)
    gqn_ref, gkn_ref, onesq_ref, onesk_ref, dup_ref, csc_ref = (
        next(it), next(it), next(it), next(it), next(it), next(it))
    if latent:
        cosd_ref, sind_ref, cosm_ref, sinm_ref, cosg_ref, sing_ref = (
            next(it), next(it), next(it), next(it), next(it), next(it))
    qd_ref, kd_ref, vdt_ref = next(it), next(it), next(it)
    qm_ref, km_ref, vmt_ref = next(it), next(it), next(it)
    qg_ref, kg_ref, vgt_ref = next(it), next(it), next(it)
    ucs_ref = next(it)
    if latent:
        qdr_ref, qmr_ref, qgr_ref = next(it), next(it), next(it)
    else:
        sdk_ref, sdv_ref, sckv_ref, skr_ref, sgk_ref, sgv_ref = (
            next(it), next(it), next(it), next(it), next(it), next(it))

    x = x_ref[...]
    mod = mod_ref[...]
    shift = mod[:, 0:D_MODEL]
    scale = mod[:, D_MODEL:2 * D_MODEL]
    h = _rms_full(x, gmix_ref[...], D_MODEL) * (1.0 + scale) + shift
    z = _dot(h.astype(BF16), win_ref[...])

    def store_t(ref, val):
        for s in range(n_sub):
            ref[s, 0] = val[s * sub_rows:(s + 1) * sub_rows, :].T.astype(BF16)

    a_q = z[:, C_AQ:C_AQ + GROUP_W]
    a_k = z[:, C_AK:C_AK + GROUP_W]
    a_v = z[:, C_AV:C_AV + GROUP_W]
    qd_ref[...] = (a_q * DIFF_SCALE).astype(BF16)
    if latent:
        cosd, sind = cosd_ref[...], sind_ref[...]
        qdr_ref[...] = (_rope(a_q, cosd, sind, DIFF_QK // 4) * DIFF_SCALE).astype(BF16)
        kd_ref[...] = _rope(a_k, cosd, sind, DIFF_QK // 4).astype(BF16)
    else:
        kd_ref[...] = a_k.astype(BF16)
        sdk_ref[...] = a_k
        sdv_ref[...] = a_v
    store_t(vdt_ref, a_v)

    ucs_ref[...] = _dot(z[:, C_BU:C_BU + GROUP_W].astype(BF16), csc_ref[...]).astype(BF16)

    cqn = _rms_full(z[:, C_CQ:C_CQ + 256], gq_ref[...], MLA_Q_RANK)
    q_m = _dot(cqn.astype(BF16), wuq_ref[...])
    qm_ref[...] = (q_m * MLA_SCALE).astype(BF16)
    ckv = _rms_full(z[:, C_CKV:C_CKV + MLA_KV_RANK], gkv_ref[...], MLA_KV_RANK)
    kv = _dot(ckv.astype(BF16), wukv_ref[...])
    kr = z[:, C_CKR:C_CKR + MLA_HEAD_W]
    if latent:
        cosm, sinm = cosm_ref[...], sinm_ref[...]
        for hd in range(MLA_HEADS):
            sl = slice(hd * MLA_HEAD_W, (hd + 1) * MLA_HEAD_W)
            qmr_ref[:, sl] = (_rope(q_m[:, sl], cosm, sinm, MLA_ROPE // 4) * MLA_SCALE).astype(BF16)
        kr = _rope(kr, cosm, sinm, MLA_ROPE // 4)
    else:
        sckv_ref[...] = ckv
        skr_ref[...] = kr[:, KR_LANE:KR_LANE + MLA_ROPE]
    for hd in range(MLA_HEADS):
        sl = slice(hd * MLA_HEAD_W, (hd + 1) * MLA_HEAD_W)
        km_ref[hd] = (kv[:, sl] + kr).astype(BF16)
    store_t(vmt_ref, kv[:, MLA_HEADS * MLA_HEAD_W:])

    d_q = z[:, C_DQ:C_DQ + GROUP_W]
    qn = d_q * lax.rsqrt(_group_sumsq(d_q, onesq_ref[...]) * (1.0 / GQA_DIM) + EPS) * gqn_ref[...]
    d_k = z[:, C_DK:C_DK + GQA_KV_HEADS * GQA_DIM]
    kn = d_k * lax.rsqrt(_group_sumsq(d_k, onesk_ref[...]) * (1.0 / GQA_DIM) + EPS) * gkn_ref[...]
    d_v = z[:, C_DV:C_DV + GQA_KV_HEADS * GQA_DIM]
    qg_ref[...] = (qn * GQA_SCALE).astype(BF16)
    if latent:
        cosg, sing = cosg_ref[...], sing_ref[...]
        qgr_ref[...] = (_rope(qn, cosg, sing, GQA_DIM // 4) * GQA_SCALE).astype(BF16)
        k_att = _rope(kn, cosg[:, :GQA_KV_HEADS * GQA_DIM], sing[:, :GQA_KV_HEADS * GQA_DIM], GQA_DIM // 4)
    else:
        sgk_ref[...] = kn
        sgv_ref[...] = d_v
        k_att = kn
    kg_ref[...] = _dot(k_att.astype(BF16), dup_ref[...]).astype(BF16)
    store_t(vgt_ref, d_v)


def _pre(x2d, mods3, mod_row_fn, lw, consts, seq, latent, rope_tabs=None):
    n_tok = x2d.shape[0]
    tb = TOKEN_BLOCK
    n_blk = n_tok // tb
    n_bat = n_tok // seq
    if seq >= tb:
        n_sub, sub_rows = 1, tb
        blk_per_seq = seq // tb
        vt_shape = lambda r: (n_bat, blk_per_seq, r, tb)
        vt_spec = lambda r: pl.BlockSpec((1, 1, r, tb), lambda i: (i // blk_per_seq, i % blk_per_seq, 0, 0))
    else:
        n_sub, sub_rows = tb // seq, seq
        blk_per_seq = 1
        vt_shape = lambda r: (n_bat, 1, r, seq)
        vt_spec = lambda r: pl.BlockSpec((n_sub, 1, r, seq), lambda i: (i, 0, 0, 0))

    def full(a):
        nd = a.ndim
        return pl.BlockSpec(a.shape, lambda i, _n=nd: (0,) * _n)

    tok = lambda w: pl.BlockSpec((tb, w), lambda i: (i, 0))
    tok_shape = lambda w, dt: jax.ShapeDtypeStruct((n_tok, w), dt)

    inputs = [x2d, mods3, lw["gmix"], lw["w_in"], lw["gq"], lw["w_uq"], lw["gkv"], lw["w_ukv"],
              lw["gqn"], lw["gkn"], consts["ones_q"], consts["ones_k"], consts["dup"], consts["csc"]]
    in_specs = [tok(D_MODEL),
                pl.BlockSpec((None, 1, N_MOD * D_MODEL), lambda i: (mod_row_fn(i), 0, 0))]
    in_specs += [full(a) for a in inputs[2:]]
    if latent:
        for t in rope_tabs:
            inputs.append(t)
            in_specs.append(pl.BlockSpec((tb, t.shape[1]), lambda i: (i % blk_per_seq, 0)))

    names = ["qd", "kd", "vdt", "qm", "km", "vmt", "qg", "kg", "vgt", "ucs"]
    km_shape = jax.ShapeDtypeStruct((MLA_HEADS, n_tok, MLA_HEAD_W), BF16)
    km_spec = pl.BlockSpec((MLA_HEADS, tb, MLA_HEAD_W), lambda i: (0, i, 0))
    out_shape = [tok_shape(256, BF16), tok_shape(256, BF16), jax.ShapeDtypeStruct(vt_shape(256), BF16),
                 tok_shape(512, BF16), km_shape, jax.ShapeDtypeStruct(vt_shape(256), BF16),
                 tok_shape(256, BF16), tok_shape(256, BF16), jax.ShapeDtypeStruct(vt_shape(128), BF16),
                 tok_shape(512, BF16)]
    out_specs = [tok(256), tok(256), vt_spec(256), tok(512), km_spec, vt_spec(256),
                 tok(256), tok(256), vt_spec(128), tok(512)]
    if latent:
        names += ["qd_rot", "qm_rot", "qg_rot"]
        out_shape += [tok_shape(256, BF16), tok_shape(512, BF16), tok_shape(256, BF16)]
        out_specs += [tok(256), tok(512), tok(256)]
    else:
        names += ["s_diff_k", "s_diff_v", "s_ckv", "s_kr", "s_gqa_k", "s_gqa_v"]
        widths = [256, 256, MLA_KV_RANK, MLA_ROPE, 128, 128]
        out_shape += [tok_shape(w, F32) for w in widths]
        out_specs += [tok(w) for w in widths]

    outs = pl.pallas_call(
        functools.partial(_pre_kernel, latent, n_sub, sub_rows),
        grid=(n_blk,),
        in_specs=in_specs,
        out_specs=out_specs,
        out_shape=out_shape,
        compiler_params=_params(1),
        name="pre_latent" if latent else "pre_context",
    )(*inputs)
    return dict(zip(names, outs))


def _cache_kernel(dk_ref, dv_ref, ckv_ref, kr_ref, gk_ref, gv_ref, wukv_ref, place_ref, dup_ref,
                  kd_ref, vdt_ref, km_ref, vmt_ref, kg_ref, vgt_ref):
    kd_ref[...] = dk_ref[...].astype(BF16)
    vdt_ref[...] = dv_ref[...].T.astype(BF16)
    kv = _dot(ckv_ref[...].astype(BF16), wukv_ref[...])
    kr4 = _dot(kr_ref[...].astype(BF16), place_ref[...])
    n_k = MLA_HEADS * MLA_HEAD_W
    for hd in range(MLA_HEADS):
        sl = slice(hd * MLA_HEAD_W, (hd + 1) * MLA_HEAD_W)
        km_ref[hd] = (kv[:, sl] + kr4[:, sl]).astype(BF16)
    vmt_ref[...] = kv[:, n_k:].T.astype(BF16)
    kg_ref[...] = _dot(gk_ref[...].astype(BF16), dup_ref[...]).astype(BF16)
    vgt_ref[...] = gv_ref[...].T.astype(BF16)


def _cache_prep(cache_diff_k, cache_diff_v, cache_mla_ckv, cache_mla_krope, cache_gqa_k, cache_gqa_v,
                w_ukv_all, consts):
    db, depth, past = cache_diff_k.shape[:3]
    ins = [cache_diff_k.reshape(db, depth, past, 256), cache_diff_v.reshape(db, depth, past, 256),
           cache_mla_ckv, cache_mla_krope, cache_gqa_k.reshape(db, depth, past, 128),
           cache_gqa_v.reshape(db, depth, past, 128)]
    cspec = lambda w: pl.BlockSpec((None, None, past, w), lambda l, b: (b, l, 0, 0))
    in_specs = [cspec(a.shape[-1]) for a in ins]
    in_specs += [pl.BlockSpec((None,) + w_ukv_all.shape[1:], lambda l, b: (l, 0, 0)),
                 pl.BlockSpec(consts["place_kr"].shape, lambda l, b: (0, 0)),
                 pl.BlockSpec(consts["dup"].shape, lambda l, b: (0, 0))]
    kspec = lambda w: pl.BlockSpec((None, None, past, w), lambda l, b: (l, b, 0, 0))
    vspec = lambda r: pl.BlockSpec((None, None, r, past), lambda l, b: (l, b, 0, 0))
    kshape = lambda w: jax.ShapeDtypeStruct((depth, db, past, w), BF16)
    vshape = lambda r: jax.ShapeDtypeStruct((depth, db, r, past), BF16)
    outs = pl.pallas_call(
        _cache_kernel,
        grid=(depth, db),
        in_specs=in_specs,
        out_specs=[kspec(256), vspec(256),
                   pl.BlockSpec((None, None, MLA_HEADS, past, MLA_HEAD_W), lambda l, b: (l, b, 0, 0, 0)),
                   vspec(256), kspec(256), vspec(128)],
        out_shape=[kshape(256), vshape(256),
                   jax.ShapeDtypeStruct((depth, db, MLA_HEADS, past, MLA_HEAD_W), BF16),
                   vshape(256), kshape(256), vshape(128)],
        compiler_params=_params(2),
        name="cache_prep",
    )(*ins, w_ukv_all, consts["place_kr"], consts["dup"])
    return dict(zip(["kd", "vdt", "km", "vmt", "kg", "vgt"], outs))


def _aligned_ds(start, size):
    if isinstance(start, int):
        return pl.ds(start, size)
    return pl.ds(pl.multiple_of(start, size), size)


def _attn_kernel(group, n_parts, chunk_counts, lam_init, *refs):
    it = iter(refs)
    q_refs = [next(it) for _ in range(n_parts)]
    kv_refs = [(next(it), next(it)) for _ in range(n_parts)]
    if group == "diff":
        lam_ref, g_ref = next(it), next(it)
    o_ref = next(it)
    m_sc, acc_sc, qe_sc = next(it), next(it), next(it)
    s_bufs = [next(it) for _ in range(PARK_BUFFERS)]
    mc_bufs = [next(it) for _ in range(PARK_BUFFERS)]

    n_heads = {"diff": 2 * DIFF_HEADS, "mla": MLA_HEADS, "gqa": GQA_HEADS}[group]
    head_shift = n_heads.bit_length() - 1
    tq = o_ref.shape[0]

    m_sc[...] = jnp.full(m_sc.shape, -jnp.inf, F32)
    acc_sc[...] = jnp.zeros(acc_sc.shape, F32)
    for part in range(n_parts):
        if group == "mla":
            for hd in range(n_heads):
                qe_sc[part, hd] = q_refs[part][:, hd * MLA_HEAD_W:(hd + 1) * MLA_HEAD_W]
        else:
            q = q_refs[part][...].astype(F32)
            lane = lax.broadcasted_iota(jnp.int32, q.shape, 1)
            head_w = q.shape[1] // n_heads
            for hd in range(n_heads):
                keep = (lane >= hd * head_w) & (lane < (hd + 1) * head_w)
                qe_sc[part, hd] = jnp.where(keep, q, 0.0).astype(BF16)

    def split(item):
        return item >> head_shift, item & (n_heads - 1)

    def scores(part, item, slot):
        k_ref = kv_refs[part][0]
        chunk = chunk_counts[part]
        c, hd = split(item)
        rows = _aligned_ds(c * chunk, chunk)
        k_blk = k_ref[hd, rows, :] if group == "mla" else k_ref[rows, :]
        s = _dot_nt(k_blk, qe_sc[part, hd])
        s_bufs[slot][0:chunk, :] = s
        mc_bufs[slot][...] = jnp.max(s, axis=0, keepdims=True)

    def accumulate(part, item, slot):
        vt_ref = kv_refs[part][1]
        chunk = chunk_counts[part]
        c, hd = split(item)
        v_row = (hd if group == "mla" else hd >> 1) * 64
        vt_blk = vt_ref[c, _aligned_ds(v_row, 64), :]
        lhs = jnp.concatenate([vt_blk, jnp.ones((ONES_ROWS, chunk), BF16)], axis=0)
        m_old = m_sc[hd]
        m_new = jnp.maximum(m_old, mc_bufs[slot][...])
        alpha = jnp.exp2(m_old - m_new)
        p = jnp.exp2(s_bufs[slot][0:chunk, :] - m_new)
        acc_sc[hd] = alpha * acc_sc[hd] + _dot(lhs, p.astype(BF16))
        m_sc[hd] = m_new

    grp = PARK_BUFFERS // 2

    def group_scores(part, g, bset):
        for k in range(grp):
            scores(part, grp * g + k, grp * bset + k)

    def group_accumulate(part, g, bset):
        for k in range(grp):
            accumulate(part, grp * g + k, grp * bset + k)

    n_groups = [kv_refs[part][1].shape[0] * n_heads // grp for part in range(n_parts)]
    assert all(n * grp == kv_refs[part][1].shape[0] * n_heads for part, n in enumerate(n_groups))
    last = n_parts - 1
    inline = [(part, g) for part in range(last) for g in range(n_groups[part])]
    off = len(inline) % 2
    group_scores(*(inline[0] if inline else (last, 0)), 0)
    for idx, (part, g) in enumerate(inline):
        nxt = inline[idx + 1] if idx + 1 < len(inline) else (last, 0)
        group_scores(*nxt, (idx + 1) % 2)
        group_accumulate(part, g, idx % 2)

    def two_steps(j, carry):
        group_scores(last, 2 * j + 1, 1 - off)
        group_accumulate(last, 2 * j, off)
        group_scores(last, 2 * j + 2, off)
        group_accumulate(last, 2 * j + 1, 1 - off)
        return carry

    n_loop = (n_groups[last] - 1) // 2
    if n_loop > 1:
        lax.fori_loop(0, n_loop, two_steps, 0)
    else:
        for j in range(n_loop):
            two_steps(j, 0)
    for g in range(2 * n_loop, n_groups[last]):
        if g + 1 < n_groups[last]:
            group_scores(last, g + 1, (off + g + 1) % 2)
        group_accumulate(last, g, (off + g) % 2)

    outs = [acc_sc[hd, 0:64, :] / acc_sc[hd, 64:65, :] for hd in range(n_heads)]

    if group == "diff":
        lp = lam_ref[...]
        lam = (jnp.exp(jnp.sum(lp[0:1] * lp[1:2], axis=-1, keepdims=True))
               - jnp.exp(jnp.sum(lp[2:3] * lp[3:4], axis=-1, keepdims=True)) + lam_init)
        g = g_ref[...]
        comb = []
        for hd in range(DIFF_HEADS):
            y = outs[2 * hd] - lam * outs[2 * hd + 1]
            ms = jnp.sum(y * y, axis=0, keepdims=True) * (1.0 / DIFF_V)
            comb.append(y * lax.rsqrt(ms + EPS) * g * (1.0 - lam_init))
        outs = comb
    o_ref[...] = jnp.concatenate(outs, axis=0).T.astype(o_ref.dtype)


def _attention(group, q_list, kv_list, n_bat, seq, lam_init=0.0, extras=()):
    tq = min(Q_TILE, seq)
    n_q = seq // tq
    in_specs, inputs, chunk_counts = [], [], []
    for q in q_list:
        inputs.append(q)
        in_specs.append(pl.BlockSpec((tq, q.shape[1]), lambda b, i: (b * n_q + i, 0)))
    for k, vt, kspec, vspec, chunk in kv_list:
        inputs += [k, vt]
        in_specs += [kspec, vspec]
        chunk_counts.append(chunk)
    for e in extras:
        inputs.append(e)
        in_specs.append(pl.BlockSpec(e.shape, lambda b, i, _n=e.ndim: (0,) * _n))
    n_heads = {"diff": 2 * DIFF_HEADS, "mla": MLA_HEADS, "gqa": GQA_HEADS}[group]
    k_width = MLA_HEAD_W if group == "mla" else q_list[0].shape[1]
    max_chunk = max(chunk_counts)
    scratch = [pltpu.VMEM((n_heads, 1, tq), F32),
               pltpu.VMEM((n_heads, 64 + ONES_ROWS, tq), F32),
               pltpu.VMEM((len(q_list), n_heads, tq, k_width), BF16)]
    scratch += [pltpu.VMEM((max_chunk, tq), F32)] * PARK_BUFFERS
    scratch += [pltpu.VMEM((1, tq), F32)] * PARK_BUFFERS
    return pl.pallas_call(
        functools.partial(_attn_kernel, group, len(q_list), tuple(chunk_counts), lam_init),
        grid=(n_bat, n_q),
        in_specs=in_specs,
        out_specs=pl.BlockSpec((tq, GROUP_W), lambda b, i: (b * n_q + i, 0)),
        out_shape=jax.ShapeDtypeStruct((n_bat * seq, GROUP_W), BF16),
        scratch_shapes=scratch,
        compiler_params=_params(2),
        name="attn_" + group,
    )(*inputs)


def _own_kv(k, vt, seq):
    if k.ndim == 3:
        kspec = pl.BlockSpec((k.shape[0], seq, k.shape[2]), lambda b, i: (0, b, 0))
    else:
        kspec = pl.BlockSpec((seq, k.shape[1]), lambda b, i: (b, 0))
    vspec = pl.BlockSpec((None,) + vt.shape[1:], lambda b, i: (b, 0, 0, 0))
    return (k, vt, kspec, vspec, vt.shape[3])


def _cache_kv(k, vt, layer):
    past = vt.shape[3]
    vt4 = vt.reshape(vt.shape[0], vt.shape[1], 1, vt.shape[2], past)
    kspec = pl.BlockSpec((None, None) + k.shape[2:], lambda b, i, _n=k.ndim - 2: (layer, b) + (0,) * _n)
    vspec = pl.BlockSpec((None, None, 1, vt.shape[2], past), lambda b, i: (layer, b, 0, 0, 0))
    return (k, vt4, kspec, vspec, past)


def _fourier_kernel(n_bat, cos_ref, sin_ref, ucs_ref, o_ref):
    cs = cos_ref[...]
    sn = sin_ref[...]
    for b in range(n_bat):
        u = ucs_ref[b]
        y = _dot(cs, u[:, :GROUP_W]) + _dot(sn, u[:, GROUP_W:])
        o_ref[b] = y.astype(o_ref.dtype)


def _fourier(ucs, n_bat, seq, cos_tab, sin_tab):
    tm = min(FOURIER_ROWS, seq)
    u3 = ucs.reshape(n_bat, seq, 2 * GROUP_W)
    out = pl.pallas_call(
        functools.partial(_fourier_kernel, n_bat),
        grid=(seq // tm,),
        in_specs=[pl.BlockSpec((tm, seq), lambda i: (i, 0)),
                  pl.BlockSpec((tm, seq), lambda i: (i, 0)),
                  pl.BlockSpec((n_bat, seq, 2 * GROUP_W), lambda i: (0, 0, 0))],
        out_specs=pl.BlockSpec((n_bat, tm, GROUP_W), lambda i: (0, i, 0)),
        out_shape=jax.ShapeDtypeStruct((n_bat, seq, GROUP_W), BF16),
        compiler_params=_params(1),
        name="fourier",
    )(cos_tab, sin_tab, u3)
    return out.reshape(n_bat * seq, GROUP_W)


def _post_kernel(final, x_ref, od_ref, yf_ref, om_ref, og_ref, mod_ref, wout_ref, gmlp_ref,
                 w1_ref, w2_ref, gfin_ref, o_ref):
    mod = mod_ref[...]
    gate1 = mod[:, 2 * D_MODEL:3 * D_MODEL]
    shift2 = mod[:, 3 * D_MODEL:4 * D_MODEL]
    scale2 = mod[:, 4 * D_MODEL:5 * D_MODEL]
    gate2 = mod[:, 5 * D_MODEL:6 * D_MODEL]
    mix = (_dot(od_ref[...], wout_ref[0]) + _dot(yf_ref[...], wout_ref[1])
           + _dot(om_ref[...], wout_ref[2]) + _dot(og_ref[...], wout_ref[3]))
    x1 = x_ref[...] + gate1 * mix
    h = (_rms_full(x1, gmlp_ref[...], D_MODEL) * (1.0 + scale2) + shift2).astype(BF16)
    acc = jnp.zeros(x1.shape, F32)
    for c in range(D_FF // MLP_CHUNK):
        sl = slice(c * MLP_CHUNK, (c + 1) * MLP_CHUNK)
        a = jnp.maximum(_dot(h, w1_ref[:, sl]), 0.0)
        acc = acc + _dot((a * a).astype(BF16), w2_ref[sl, :])
    x2 = x1 + gate2 * acc
    if final:
        x2 = _rms_full(x2, gfin_ref[...], D_MODEL)
    o_ref[...] = x2


def _post(x2d, o_diff, y_f, o_mla, o_gqa, mods3, mod_row_fn, lw, g_final, final, name):
    n_tok = x2d.shape[0]
    tb = TOKEN_BLOCK
    tok = lambda w: pl.BlockSpec((tb, w), lambda i: (i, 0))
    const = lambda a: pl.BlockSpec(a.shape, lambda i, _n=a.ndim: (0,) * _n)
    return pl.pallas_call(
        functools.partial(_post_kernel, final),
        grid=(n_tok // tb,),
        in_specs=[tok(D_MODEL), tok(GROUP_W), tok(GROUP_W), tok(GROUP_W), tok(GROUP_W),
                  pl.BlockSpec((None, 1, N_MOD * D_MODEL), lambda i: (mod_row_fn(i), 0, 0)),
                  const(lw["w_out"]), const(lw["gmlp"]), const(lw["w1"]), const(lw["w2"]), const(g_final)],
        out_specs=tok(D_MODEL),
        out_shape=jax.ShapeDtypeStruct((n_tok, D_MODEL), F32),
        compiler_params=_params(1),
        name=name,
    )(x2d, o_diff, y_f, o_mla, o_gqa, mods3, lw["w_out"], lw["gmlp"], lw["w1"], lw["w2"], g_final)


def _rope_tables(n_lat, head_dim, width, lane0=0):
    a = head_dim // 2
    inv = ROPE_THETA ** (-jnp.arange(0, a, 2, dtype=F32) / a)
    pos = jnp.arange(n_lat, dtype=jnp.int32)
    row = (pos // GRID_W).astype(F32)
    col = (pos % GRID_W).astype(F32)
    lane = jnp.arange(width, dtype=jnp.int32) - lane0
    in_pat = (lane >= 0) & ((lane < head_dim) | (lane0 == 0))
    li = jnp.where(in_pat, lane % head_dim, 0)
    part = li // a
    ii = li % a
    first = (ii // (a // 2)) == 0
    f = ii % (a // 2)
    p = jnp.where(part[None, :] == 0, row[:, None], col[:, None])
    ang = p * inv[f][None, :]
    cos = jnp.where(in_pat[None, :], jnp.cos(ang), 1.0)
    ssin = jnp.where(in_pat[None, :], jnp.where(first[None, :], -jnp.sin(ang), jnp.sin(ang)), 0.0)
    return cos, ssin


def _dft_tables(n):
    s = n ** -0.5
    k = jnp.arange(n, dtype=jnp.int32)

    def cos_sin(rows, scale_rows):
        r = jnp.arange(rows, dtype=jnp.int32) * scale_rows
        ang = ((r[:, None] * k[None, :]) % n).astype(F32) * (2.0 * math.pi / n)
        return jnp.cos(ang), jnp.sin(ang)

    if n <= 4 * GRID_W:
        c, sn = cos_sin(n, 1)
        return (c * s).astype(BF16), (sn * s).astype(BF16)
    n1 = GRID_W
    ca, sa = cos_sin(n // n1, n1)
    cb, sb = cos_sin(n1, 1)
    cos = ca[:, None, :] * cb[None, :, :] - sa[:, None, :] * sb[None, :, :]
    sin = sa[:, None, :] * cb[None, :, :] + ca[:, None, :] * sb[None, :, :]
    return (cos * s).astype(BF16).reshape(n, n), (sin * s).astype(BF16).reshape(n, n)


def _constants():
    eye = lambda n: jnp.eye(n, dtype=F32)
    ones_q = jnp.kron(eye(GQA_HEADS), jnp.ones((GQA_DIM, GQA_DIM), F32)).astype(BF16)
    ones_k = jnp.kron(eye(GQA_KV_HEADS), jnp.ones((GQA_DIM, GQA_DIM), F32)).astype(BF16)
    dup = jnp.kron(eye(GQA_KV_HEADS), jnp.concatenate([eye(GQA_DIM), eye(GQA_DIM)], axis=1)).astype(BF16)
    one_head = jnp.concatenate([jnp.zeros((MLA_ROPE, KR_LANE), F32), eye(MLA_ROPE),
                                jnp.zeros((MLA_ROPE, MLA_HEAD_W - KR_LANE - MLA_ROPE), F32)], axis=1)
    place_kr = jnp.tile(one_head, (1, MLA_HEADS)).astype(BF16)
    j = jnp.arange(FNET_CH, dtype=jnp.int32)
    ang = ((j[:, None] * j[None, :]) % FNET_CH).astype(F32) * (2.0 * math.pi / FNET_CH)
    cc = jnp.kron(eye(FNET_GROUPS), jnp.cos(ang) * FNET_CH ** -0.5)
    sc = jnp.kron(eye(FNET_GROUPS), jnp.sin(ang) * FNET_CH ** -0.5)
    csc = jnp.concatenate([cc, -sc], axis=1).astype(BF16)
    return dict(ones_q=ones_q, ones_k=ones_k, dup=dup, place_kr=place_kr, csc=csc)


def _layer_weights(l, norm_mix_g, norm_mlp_g, w_in, mla_q_norm_g, mla_w_uq, mla_kv_norm_g, mla_w_ukv,
                   gqa_q_norm_g, gqa_k_norm_g, w_out, mlp_w1, mlp_w2):
    wi = w_in[l]
    zc = lambda n: jnp.zeros((D_MODEL, n), F32)
    off = [0, 256, 512, 768, 1024, 1216, 1344, 1376, 1632, 1760, 1888]
    seg = [wi[:, off[k]:off[k + 1]] for k in range(10)]
    w_in_p = jnp.concatenate([
        seg[0], seg[1], seg[2], seg[3],
        seg[4], zc(256 - MLA_Q_RANK),
        seg[5],
        zc(KR_LANE), seg[6], zc(MLA_HEAD_W - KR_LANE - MLA_ROPE),
        seg[7], seg[8], seg[9]], axis=1).astype(BF16)
    wuq = mla_w_uq[l].reshape(MLA_Q_RANK, MLA_HEADS, MLA_NOPE + MLA_ROPE)
    wuq = jnp.pad(wuq, ((0, 256 - MLA_Q_RANK), (0, 0), (0, MLA_HEAD_W - MLA_NOPE - MLA_ROPE)))
    w_uq_p = wuq.reshape(256, MLA_HEADS * MLA_HEAD_W).astype(BF16)
    wukv = mla_w_ukv[l].reshape(MLA_KV_RANK, MLA_HEADS, MLA_NOPE + MLA_V)
    wk = jnp.pad(wukv[:, :, :MLA_NOPE], ((0, 0), (0, 0), (0, MLA_HEAD_W - MLA_NOPE)))
    wv = wukv[:, :, MLA_NOPE:]
    w_ukv_p = jnp.concatenate([wk.reshape(MLA_KV_RANK, -1), wv.reshape(MLA_KV_RANK, -1)], axis=1).astype(BF16)
    return dict(
        gmix=norm_mix_g[l][None, :], gmlp=norm_mlp_g[l][None, :], w_in=w_in_p,
        gq=jnp.pad(mla_q_norm_g[l], (0, 256 - MLA_Q_RANK))[None, :], w_uq=w_uq_p,
        gkv=mla_kv_norm_g[l][None, :], w_ukv=w_ukv_p,
        gqn=jnp.tile(gqa_q_norm_g[l], GQA_HEADS)[None, :], gkn=jnp.tile(gqa_k_norm_g[l], GQA_KV_HEADS)[None, :],
        w_out=w_out[l].reshape(4, GROUP_W, D_MODEL).astype(BF16),
        w1=mlp_w1[l].astype(BF16), w2=mlp_w2[l].astype(BF16))


def kernel(x_prompt, x_sample, cache_diff_k, cache_diff_v, cache_mla_ckv, cache_mla_krope, cache_gqa_k, cache_gqa_v, c, c_ctx, norm_mix_g, norm_mlp_g, ada_w, ada_b, w_in, diff_lambda, diff_subln_g, mla_q_norm_g, mla_w_uq, mla_kv_norm_g, mla_w_ukv, gqa_q_norm_g, gqa_k_norm_g, w_out, mlp_w1, mlp_w2, final_norm_g):
    n_ctx_b, seq, _ = x_prompt.shape
    n_lat_b, n_lat, _ = x_sample.shape
    depth = w_in.shape[0]
    assert seq % 128 == 0 and TOKEN_BLOCK % seq == 0 and (n_ctx_b * seq) % TOKEN_BLOCK == 0
    assert n_lat % TOKEN_BLOCK == 0 and n_lat % KEY_CHUNK == 0 and n_lat % GRID_W == 0

    consts = _constants()
    rows = 1 + n_lat_b
    rows_pad = -(-rows // 8) * 8
    cs = jnp.concatenate([c_ctx[None, :], c, jnp.zeros((rows_pad - rows, D_MODEL), F32)], axis=0)
    mods = _adaln(cs, ada_w, ada_b).reshape(depth * rows_pad, 1, N_MOD * D_MODEL)

    rope_tabs = (_rope_tables(n_lat, DIFF_QK, GROUP_W) + _rope_tables(n_lat, MLA_ROPE, MLA_HEAD_W, KR_LANE)
                 + _rope_tables(n_lat, GQA_DIM, GROUP_W))
    dft_ctx = _dft_tables(seq)
    dft_lat = _dft_tables(n_lat)

    layers = [_layer_weights(l, norm_mix_g, norm_mlp_g, w_in, mla_q_norm_g, mla_w_uq, mla_kv_norm_g,
                             mla_w_ukv, gqa_q_norm_g, gqa_k_norm_g, w_out, mlp_w1, mlp_w2)
              for l in range(depth)]
    cache = _cache_prep(cache_diff_k, cache_diff_v, cache_mla_ckv, cache_mla_krope, cache_gqa_k, cache_gqa_v,
                        jnp.stack([lw["w_ukv"] for lw in layers]), consts)

    g_final = final_norm_g[None, :]
    xp = x_prompt.reshape(n_ctx_b * seq, D_MODEL)
    xs = x_sample.reshape(n_lat_b * n_lat, D_MODEL)
    lat_blocks = n_lat // TOKEN_BLOCK
    states = [[] for _ in range(6)]
    for l in range(depth):
        lw = layers[l]
        lam_init = 0.8 - 0.6 * math.exp(-0.3 * l)
        ctx_row = lambda i, _l=l: _l * rows_pad
        lat_row = lambda i, _l=l: _l * rows_pad + 1 + i // lat_blocks
        lam_p = diff_lambda[l].astype(F32)
        final = l == depth - 1

        t = _pre(xp, mods, ctx_row, lw, consts, seq, latent=False)
        g_t = jnp.broadcast_to(diff_subln_g[l][:, None], (DIFF_V, min(Q_TILE, seq)))
        o_a = _attention("diff", [t["qd"]], [_own_kv(t["kd"], t["vdt"], seq)], n_ctx_b, seq, lam_init,
                         extras=(lam_p, g_t))
        o_c = _attention("mla", [t["qm"]], [_own_kv(t["km"], t["vmt"], seq)], n_ctx_b, seq)
        o_d = _attention("gqa", [t["qg"]], [_own_kv(t["kg"], t["vgt"], seq)], n_ctx_b, seq)
        y_f = _fourier(t["ucs"], n_ctx_b, seq, *dft_ctx)
        xp = _post(xp, o_a, y_f, o_c, o_d, mods, ctx_row, lw, g_final, final, "post_context")
        for k, nm in enumerate(["s_diff_k", "s_diff_v", "s_ckv", "s_kr", "s_gqa_k", "s_gqa_v"]):
            states[k].append(t[nm])

        t = _pre(xs, mods, lat_row, lw, consts, n_lat, latent=True, rope_tabs=rope_tabs)
        g_t = jnp.broadcast_to(diff_subln_g[l][:, None], (DIFF_V, min(Q_TILE, n_lat)))
        o_a = _attention("diff", [t["qd"], t["qd_rot"]],
                         [_cache_kv(cache["kd"], cache["vdt"], l), _own_kv(t["kd"], t["vdt"], n_lat)],
                         n_lat_b, n_lat, lam_init, extras=(lam_p, g_t))
        o_c = _attention("mla", [t["qm"], t["qm_rot"]],
                         [_cache_kv(cache["km"], cache["vmt"], l), _own_kv(t["km"], t["vmt"], n_lat)],
                         n_lat_b, n_lat)
        o_d = _attention("gqa", [t["qg"], t["qg_rot"]],
                         [_cache_kv(cache["kg"], cache["vgt"], l), _own_kv(t["kg"], t["vgt"], n_lat)],
                         n_lat_b, n_lat)
        y_f = _fourier(t["ucs"], n_lat_b, n_lat, *dft_lat)
        xs = _post(xs, o_a, y_f, o_c, o_d, mods, lat_row, lw, g_final, final, "post_latent")

    y_prompt = xp.reshape(n_ctx_b, seq, D_MODEL)
    y_sample = xs.reshape(n_lat_b, n_lat, D_MODEL)
    st = [jnp.stack([a.reshape(n_ctx_b, seq, -1) for a in s], axis=1) for s in states]
    new_diff_k = st[0].reshape(n_ctx_b, depth, seq, DIFF_HEADS, 2 * DIFF_QK)
    new_diff_v = st[1].reshape(n_ctx_b, depth, seq, DIFF_HEADS, DIFF_V)
    new_gqa_k = st[4].reshape(n_ctx_b, depth, seq, GQA_KV_HEADS, GQA_DIM)
    new_gqa_v = st[5].reshape(n_ctx_b, depth, seq, GQA_KV_HEADS, GQA_DIM)
    return (y_prompt, y_sample, new_diff_k, new_diff_v, st[2], st[3], new_gqa_k, new_gqa_v)
```

```python
import functools
import math

import jax
import jax.numpy as jnp
from jax import lax
from jax.experimental import pallas as pl
from jax.experimental.pallas import tpu as pltpu

F32 = jnp.float32
BF16 = jnp.bfloat16

D_MODEL = 1024
GRID_W = 64
GROUP_W = 256
DIFF_HEADS = 4
DIFF_QK = 32
DIFF_V = 64
FNET_GROUPS = 4
FNET_CH = 64
MLA_HEADS = 4
MLA_Q_RANK = 192
MLA_KV_RANK = 128
MLA_NOPE = 64
MLA_ROPE = 32
MLA_V = 64
MLA_HEAD_W = 128
GQA_HEADS = 4
GQA_KV_HEADS = 2
GQA_DIM = 64
D_FF = 4 * D_MODEL
N_MOD = 6
ROPE_THETA = 10000.0
EPS = 1e-6
LOG2E = math.log2(math.e)
DIFF_SCALE = DIFF_QK ** -0.5 * LOG2E
MLA_SCALE = (MLA_NOPE + MLA_ROPE) ** -0.5 * LOG2E
GQA_SCALE = GQA_DIM ** -0.5 * LOG2E
ONES_ROWS = 16
PARK_BUFFERS = 8

V7X_VMEM_LIMIT_BYTES = 56 * 1024 * 1024

C_AQ, C_AK, C_AV, C_BU, C_CQ, C_CKV, C_CKR, C_DQ, C_DK, C_DV, IN_W_PAD = (
    0, 256, 512, 768, 1024, 1280, 1408, 1536, 1792, 1920, 2048)
KR_LANE = MLA_NOPE

TOKEN_BLOCK = 512
KEY_CHUNK = 512
Q_TILE = 1024
MLP_CHUNK = 1024
ADA_TILE = 1536
FOURIER_ROWS = 256


def _params(n_axes):
    return pltpu.CompilerParams(dimension_semantics=("arbitrary",) * n_axes,
                                vmem_limit_bytes=V7X_VMEM_LIMIT_BYTES)


def _dot(a, b):
    return jnp.dot(a, b, preferred_element_type=F32)


def _dot_nt(a, b):
    return lax.dot_general(a, b, (((1,), (1,)), ((), ())), preferred_element_type=F32)


def _adaln_kernel(c_ref, w_ref, b_ref, o_ref):
    c = c_ref[...]
    s = c / (1.0 + jnp.exp(-c))
    o_ref[...] = _dot(s.astype(BF16), w_ref[...].astype(BF16)) + b_ref[...]


def _adaln(cs, ada_w, ada_b):
    depth = ada_w.shape[0]
    rows = cs.shape[0]
    n_out = ada_w.shape[2]
    return pl.pallas_call(
        _adaln_kernel,
        grid=(depth, n_out // ADA_TILE),
        in_specs=[
            pl.BlockSpec((rows, D_MODEL), lambda l, j: (0, 0)),
            pl.BlockSpec((None, D_MODEL, ADA_TILE), lambda l, j: (l, 0, j)),
            pl.BlockSpec((None, 1, ADA_TILE), lambda l, j: (l, 0, j)),
        ],
        out_specs=pl.BlockSpec((None, rows, ADA_TILE), lambda l, j: (l, 0, j)),
        out_shape=jax.ShapeDtypeStruct((depth, rows, n_out), F32),
        compiler_params=_params(2),
        name="adaln",
    )(cs, ada_w, ada_b.reshape(depth, 1, n_out))


def _rms_full(x, g, n):
    ss = jnp.sum(x * x, axis=-1, keepdims=True)
    return x * lax.rsqrt(ss * (1.0 / n) + EPS) * g


def _group_sumsq(x, ones_bd):
    sq = x * x
    hi = sq.astype(BF16)
    lo = (sq - hi.astype(F32)).astype(BF16)
    return _dot(hi, ones_bd) + _dot(lo, ones_bd)


def _rope(x, cos, ssin, half):
    width = x.shape[-1]
    lane = lax.broadcasted_iota(jnp.int32, x.shape, 1)
    first = (lane & half) == 0
    partner = jnp.where(first, pltpu.roll(x, width - half, 1), pltpu.roll(x, half, 1))
    return x * cos + partner * ssin


def _pre_kernel(latent, n_sub, sub_rows, *refs):
    it = iter(refs)
    x_ref, mod_ref, gmix_ref, win_ref = next(it), next(it), next(it), next(it)
    gq_ref, wuq_ref, gkv_ref, wukv_ref = next(it), next(it), next(it), next(it)
    gqn_ref, gkn_ref, onesq_ref, onesk_ref, dup_ref, csc_ref = (
        next(it), next(it), next(it), next(it), next(it), next(it))
    if latent:
        cosd_ref, sind_ref, cosm_ref, sinm_ref, cosg_ref, sing_ref = (
            next(it), next(it), next(it), next(it), next(it), next(it))
    qd_ref, kd_ref, vdt_ref = next(it), next(it), next(it)
    qm_ref, km_ref, vmt_ref = next(it), next(it), next(it)
    qg_ref, kg_ref, vgt_ref = next(it), next(it), next(it)
    ucs_ref = next(it)
    if latent:
        qdr_ref, qmr_ref, qgr_ref = next(it), next(it), next(it)
    else:
        sdk_ref, sdv_ref, sckv_ref, skr_ref, sgk_ref, sgv_ref = (
            next(it), next(it), next(it), next(it), next(it), next(it))

    x = x_ref[...]
    mod = mod_ref[...]
    shift = mod[:, 0:D_MODEL]
    scale = mod[:, D_MODEL:2 * D_MODEL]
    h = _rms_full(x, gmix_ref[...], D_MODEL) * (1.0 + scale) + shift
    z = _dot(h.astype(BF16), win_ref[...])

    def store_t(ref, val):
        for s in range(n_sub):
            ref[s, 0] = val[s * sub_rows:(s + 1) * sub_rows, :].T.astype(BF16)

    a_q = z[:, C_AQ:C_AQ + GROUP_W]
    a_k = z[:, C_AK:C_AK + GROUP_W]
    a_v = z[:, C_AV:C_AV + GROUP_W]
    qd_ref[...] = (a_q * DIFF_SCALE).astype(BF16)
    if latent:
        cosd, sind = cosd_ref[...], sind_ref[...]
        qdr_ref[...] = (_rope(a_q, cosd, sind, DIFF_QK // 4) * DIFF_SCALE).astype(BF16)
        kd_ref[...] = _rope(a_k, cosd, sind, DIFF_QK // 4).astype(BF16)
    else:
        kd_ref[...] = a_k.astype(BF16)
        sdk_ref[...] = a_k
        sdv_ref[...] = a_v
    store_t(vdt_ref, a_v)

    ucs_ref[...] = _dot(z[:, C_BU:C_BU + GROUP_W].astype(BF16), csc_ref[...]).astype(BF16)

    cqn = _rms_full(z[:, C_CQ:C_CQ + 256], gq_ref[...], MLA_Q_RANK)
    q_m = _dot(cqn.astype(BF16), wuq_ref[...])
    qm_ref[...] = (q_m * MLA_SCALE).astype(BF16)
    ckv = _rms_full(z[:, C_CKV:C_CKV + MLA_KV_RANK], gkv_ref[...], MLA_KV_RANK)
    kv = _dot(ckv.astype(BF16), wukv_ref[...])
    kr = z[:, C_CKR:C_CKR + MLA_HEAD_W]
    if latent:
        cosm, sinm = cosm_ref[...], sinm_ref[...]
        for hd in range(MLA_HEADS):
            sl = slice(hd * MLA_HEAD_W, (hd + 1) * MLA_HEAD_W)
            qmr_ref[:, sl] = (_rope(q_m[:, sl], cosm, sinm, MLA_ROPE // 4) * MLA_SCALE).astype(BF16)
        kr = _rope(kr, cosm, sinm, MLA_ROPE // 4)
    else:
        sckv_ref[...] = ckv
        skr_ref[...] = kr[:, KR_LANE:KR_LANE + MLA_ROPE]
    for hd in range(MLA_HEADS):
        sl = slice(hd * MLA_HEAD_W, (hd + 1) * MLA_HEAD_W)
        km_ref[hd] = (kv[:, sl] + kr).astype(BF16)
    store_t(vmt_ref, kv[:, MLA_HEADS * MLA_HEAD_W:])

    d_q = z[:, C_DQ:C_DQ + GROUP_W]
    qn = d_q * lax.rsqrt(_group_sumsq(d_q, onesq_ref[...]) * (1.0 / GQA_DIM) + EPS) * gqn_ref[...]
    d_k = z[:, C_DK:C_DK + GQA_KV_HEADS * GQA_DIM]
    kn = d_k * lax.rsqrt(_group_sumsq(d_k, onesk_ref[...]) * (1.0 / GQA_DIM) + EPS) * gkn_ref[...]
    d_v = z[:, C_DV:C_DV + GQA_KV_HEADS * GQA_DIM]
    qg_ref[...] = (qn * GQA_SCALE).astype(BF16)
    if latent:
        cosg, sing = cosg_ref[...], sing_ref[...]
        qgr_ref[...] = (_rope(qn, cosg, sing, GQA_DIM // 4) * GQA_SCALE).astype(BF16)
        k_att = _rope(kn, cosg[:, :GQA_KV_HEADS * GQA_DIM], sing[:, :GQA_KV_HEADS * GQA_DIM], GQA_DIM // 4)
    else:
        sgk_ref[...] = kn
        sgv_ref[...] = d_v
        k_att = kn
    kg_ref[...] = _dot(k_att.astype(BF16), dup_ref[...]).astype(BF16)
    store_t(vgt_ref, d_v)


def _pre(x2d, mods3, mod_row_fn, lw, consts, seq, latent, rope_tabs=None):
    n_tok = x2d.shape[0]
    tb = TOKEN_BLOCK
    n_blk = n_tok // tb
    n_bat = n_tok // seq
    if seq >= tb:
        n_sub, sub_rows = 1, tb
        blk_per_seq = seq // tb
        vt_shape = lambda r: (n_bat, blk_per_seq, r, tb)
        vt_spec = lambda r: pl.BlockSpec((1, 1, r, tb), lambda i: (i // blk_per_seq, i % blk_per_seq, 0, 0))
    else:
        n_sub, sub_rows = tb // seq, seq
        blk_per_seq = 1
        vt_shape = lambda r: (n_bat, 1, r, seq)
        vt_spec = lambda r: pl.BlockSpec((n_sub, 1, r, seq), lambda i: (i, 0, 0, 0))

    def full(a):
        nd = a.ndim
        return pl.BlockSpec(a.shape, lambda i, _n=nd: (0,) * _n)

    tok = lambda w: pl.BlockSpec((tb, w), lambda i: (i, 0))
    tok_shape = lambda w, dt: jax.ShapeDtypeStruct((n_tok, w), dt)

    inputs = [x2d, mods3, lw["gmix"], lw["w_in"], lw["gq"], lw["w_uq"], lw["gkv"], lw["w_ukv"],
              lw["gqn"], lw["gkn"], consts["ones_q"], consts["ones_k"], consts["dup"], consts["csc"]]
    in_specs = [tok(D_MODEL),
                pl.BlockSpec((None, 1, N_MOD * D_MODEL), lambda i: (mod_row_fn(i), 0, 0))]
    in_specs += [full(a) for a in inputs[2:]]
    if latent:
        for t in rope_tabs:
            inputs.append(t)
            in_specs.append(pl.BlockSpec((tb, t.shape[1]), lambda i: (i % blk_per_seq, 0)))

    names = ["qd", "kd", "vdt", "qm", "km", "vmt", "qg", "kg", "vgt", "ucs"]
    km_shape = jax.ShapeDtypeStruct((MLA_HEADS, n_tok, MLA_HEAD_W), BF16)
    km_spec = pl.BlockSpec((MLA_HEADS, tb, MLA_HEAD_W), lambda i: (0, i, 0))
    out_shape = [tok_shape(256, BF16), tok_shape(256, BF16), jax.ShapeDtypeStruct(vt_shape(256), BF16),
                 tok_shape(512, BF16), km_shape, jax.ShapeDtypeStruct(vt_shape(256), BF16),
                 tok_shape(256, BF16), tok_shape(256, BF16), jax.ShapeDtypeStruct(vt_shape(128), BF16),
                 tok_shape(512, BF16)]
    out_specs = [tok(256), tok(256), vt_spec(256), tok(512), km_spec, vt_spec(256),
                 tok(256), tok(256), vt_spec(128), tok(512)]
    if latent:
        names += ["qd_rot", "qm_rot", "qg_rot"]
        out_shape += [tok_shape(256, BF16), tok_shape(512, BF16), tok_shape(256, BF16)]
        out_specs += [tok(256), tok(512), tok(256)]
    else:
        names += ["s_diff_k", "s_diff_v", "s_ckv", "s_kr", "s_gqa_k", "s_gqa_v"]
        widths = [256, 256, MLA_KV_RANK, MLA_ROPE, 128, 128]
        out_shape += [tok_shape(w, F32) for w in widths]
        out_specs += [tok(w) for w in widths]

    outs = pl.pallas_call(
        functools.partial(_pre_kernel, latent, n_sub, sub_rows),
        grid=(n_blk,),
        in_specs=in_specs,
        out_specs=out_specs,
        out_shape=out_shape,
        compiler_params=_params(1),
        name="pre_latent" if latent else "pre_context",
    )(*inputs)
    return dict(zip(names, outs))


def _cache_kernel(dk_ref, dv_ref, ckv_ref, kr_ref, gk_ref, gv_ref, wukv_ref, place_ref, dup_ref,
                  kd_ref, vdt_ref, km_ref, vmt_ref, kg_ref, vgt_ref):
    kd_ref[...] = dk_ref[...].astype(BF16)
    vdt_ref[...] = dv_ref[...].T.astype(BF16)
    kv = _dot(ckv_ref[...].astype(BF16), wukv_ref[...])
    kr4 = _dot(kr_ref[...].astype(BF16), place_ref[...])
    n_k = MLA_HEADS * MLA_HEAD_W
    for hd in range(MLA_HEADS):
        sl = slice(hd * MLA_HEAD_W, (hd + 1) * MLA_HEAD_W)
        km_ref[hd] = (kv[:, sl] + kr4[:, sl]).astype(BF16)
    vmt_ref[...] = kv[:, n_k:].T.astype(BF16)
    kg_ref[...] = _dot(gk_ref[...].astype(BF16), dup_ref[...]).astype(BF16)
    vgt_ref[...] = gv_ref[...].T.astype(BF16)


def _cache_prep(cache_diff_k, cache_diff_v, cache_mla_ckv, cache_mla_krope, cache_gqa_k, cache_gqa_v,
                w_ukv_all, consts):
    db, depth, past = cache_diff_k.shape[:3]
    ins = [cache_diff_k.reshape(db, depth, past, 256), cache_diff_v.reshape(db, depth, past, 256),
           cache_mla_ckv, cache_mla_krope, cache_gqa_k.reshape(db, depth, past, 128),
           cache_gqa_v.reshape(db, depth, past, 128)]
    cspec = lambda w: pl.BlockSpec((None, None, past, w), lambda l, b: (b, l, 0, 0))
    in_specs = [cspec(a.shape[-1]) for a in ins]
    in_specs += [pl.BlockSpec((None,) + w_ukv_all.shape[1:], lambda l, b: (l, 0, 0)),
                 pl.BlockSpec(consts["place_kr"].shape, lambda l, b: (0, 0)),
                 pl.BlockSpec(consts["dup"].shape, lambda l, b: (0, 0))]
    kspec = lambda w: pl.BlockSpec((None, None, past, w), lambda l, b: (l, b, 0, 0))
    vspec = lambda r: pl.BlockSpec((None, None, r, past), lambda l, b: (l, b, 0, 0))
    kshape = lambda w: jax.ShapeDtypeStruct((depth, db, past, w), BF16)
    vshape = lambda r: jax.ShapeDtypeStruct((depth, db, r, past), BF16)
    outs = pl.pallas_call(
        _cache_kernel,
        grid=(depth, db),
        in_specs=in_specs,
        out_specs=[kspec(256), vspec(256),
                   pl.BlockSpec((None, None, MLA_HEADS, past, MLA_HEAD_W), lambda l, b: (l, b, 0, 0, 0)),
                   vspec(256), kspec(256), vspec(128)],
        out_shape=[kshape(256), vshape(256),
                   jax.ShapeDtypeStruct((depth, db, MLA_HEADS, past, MLA_HEAD_W), BF16),
                   vshape(256), kshape(256), vshape(128)],
        compiler_params=_params(2),
        name="cache_prep",
    )(*ins, w_ukv_all, consts["place_kr"], consts["dup"])
    return dict(zip(["kd", "vdt", "km", "vmt", "kg", "vgt"], outs))


def _aligned_ds(start, size):
    if isinstance(start, int):
        return pl.ds(start, size)
    return pl.ds(pl.multiple_of(start, size), size)


def _attn_kernel(group, n_parts, chunk_counts, lam_init, *refs):
    it = iter(refs)
    q_refs = [next(it) for _ in range(n_parts)]
    kv_refs = [(next(it), next(it)) for _ in range(n_parts)]
    if group == "diff":
        lam_ref, g_ref = next(it), next(it)
    o_ref = next(it)
    m_sc, acc_sc, qe_sc = next(it), next(it), next(it)
    s_bufs = [next(it) for _ in range(PARK_BUFFERS)]
    mc_bufs = [next(it) for _ in range(PARK_BUFFERS)]

    n_heads = {"diff": 2 * DIFF_HEADS, "mla": MLA_HEADS, "gqa": GQA_HEADS}[group]
    head_shift = n_heads.bit_length() - 1
    tq = o_ref.shape[0]

    m_sc[...] = jnp.full(m_sc.shape, -jnp.inf, F32)
    acc_sc[...] = jnp.zeros(acc_sc.shape, F32)
    for part in range(n_parts):
        if group == "mla":
            for hd in range(n_heads):
                qe_sc[part, hd] = q_refs[part][:, hd * MLA_HEAD_W:(hd + 1) * MLA_HEAD_W]
        else:
            q = q_refs[part][...].astype(F32)
            lane = lax.broadcasted_iota(jnp.int32, q.shape, 1)
            head_w = q.shape[1] // n_heads
            for hd in range(n_heads):
                keep = (lane >= hd * head_w) & (lane < (hd + 1) * head_w)
                qe_sc[part, hd] = jnp.where(keep, q, 0.0).astype(BF16)

    def split(item):
        return item >> head_shift, item & (n_heads - 1)

    def scores(part, item, slot):
        k_ref = kv_refs[part][0]
        chunk = chunk_counts[part]
        c, hd = split(item)
        rows = _aligned_ds(c * chunk, chunk)
        k_blk = k_ref[hd, rows, :] if group == "mla" else k_ref[rows, :]
        s = _dot_nt(k_blk, qe_sc[part, hd])
        s_bufs[slot][0:chunk, :] = s
        mc_bufs[slot][...] = jnp.max(s, axis=0, keepdims=True)

    def accumulate(part, item, slot):
        vt_ref = kv_refs[part][1]
        chunk = chunk_counts[part]
        c, hd = split(item)
        v_row = (hd if group == "mla" else hd >> 1) * 64
        vt_blk = vt_ref[c, _aligned_ds(v_row, 64), :]
        lhs = jnp.concatenate([vt_blk, jnp.ones((ONES_ROWS, chunk), BF16)], axis=0)
        m_old = m_sc[hd]
        m_new = jnp.maximum(m_old, mc_bufs[slot][...])
        alpha = jnp.exp2(m_old - m_new)
        p = jnp.exp2(s_bufs[slot][0:chunk, :] - m_new)
        acc_sc[hd] = alpha * acc_sc[hd] + _dot(lhs, p.astype(BF16))
        m_sc[hd] = m_new

    grp = PARK_BUFFERS // 2

    def group_scores(part, g, bset):
        for k in range(grp):
            scores(part, grp * g + k, grp * bset + k)

    def group_accumulate(part, g, bset):
        for k in range(grp):
            accumulate(part, grp * g + k, grp * bset + k)

    def step(part_a, g_a, set_a, part_b, g_b, set_b):
        for k in range(grp):
            scores(part_a, grp * g_a + k, grp * set_a + k)
            accumulate(part_b, grp * g_b + k, grp * set_b + k)

    n_groups = [kv_refs[part][1].shape[0] * n_heads // grp for part in range(n_parts)]
    assert all(n * grp == kv_refs[part][1].shape[0] * n_heads for part, n in enumerate(n_groups))
    last = n_parts - 1
    inline = [(part, g) for part in range(last) for g in range(n_groups[part])]
    off = len(inline) % 2
    group_scores(*(inline[0] if inline else (last, 0)), 0)
    for idx, (part, g) in enumerate(inline):
        nxt = inline[idx + 1] if idx + 1 < len(inline) else (last, 0)
        step(*nxt, (idx + 1) % 2, part, g, idx % 2)

    def two_steps(j, carry):
        step(last, 2 * j + 1, 1 - off, last, 2 * j, off)
        step(last, 2 * j + 2, off, last, 2 * j + 1, 1 - off)
        return carry

    n_loop = (n_groups[last] - 1) // 2
    if n_loop > 1:
        lax.fori_loop(0, n_loop, two_steps, 0)
    else:
        for j in range(n_loop):
            two_steps(j, 0)
    for g in range(2 * n_loop, n_groups[last]):
        if g + 1 < n_groups[last]:
            step(last, g + 1, (off + g + 1) % 2, last, g, (off + g) % 2)
        else:
            group_accumulate(last, g, (off + g) % 2)

    outs = [acc_sc[hd, 0:64, :] / acc_sc[hd, 64:65, :] for hd in range(n_heads)]

    if group == "diff":
        lp = lam_ref[...]
        lam = (jnp.exp(jnp.sum(lp[0:1] * lp[1:2], axis=-1, keepdims=True))
               - jnp.exp(jnp.sum(lp[2:3] * lp[3:4], axis=-1, keepdims=True)) + lam_init)
        g = g_ref[...]
        comb = []
        for hd in range(DIFF_HEADS):
            y = outs[2 * hd] - lam * outs[2 * hd + 1]
            ms = jnp.sum(y * y, axis=0, keepdims=True) * (1.0 / DIFF_V)
            comb.append(y * lax.rsqrt(ms + EPS) * g * (1.0 - lam_init))
        outs = comb
    o_ref[...] = jnp.concatenate(outs, axis=0).T.astype(o_ref.dtype)


def _attention(group, q_list, kv_list, n_bat, seq, lam_init=0.0, extras=()):
    tq = min(Q_TILE, seq)
    n_q = seq // tq
    in_specs, inputs, chunk_counts = [], [], []
    for q in q_list:
        inputs.append(q)
        in_specs.append(pl.BlockSpec((tq, q.shape[1]), lambda b, i: (b * n_q + i, 0)))
    for k, vt, kspec, vspec, chunk in kv_list:
        inputs += [k, vt]
        in_specs += [kspec, vspec]
        chunk_counts.append(chunk)
    for e in extras:
        inputs.append(e)
        in_specs.append(pl.BlockSpec(e.shape, lambda b, i, _n=e.ndim: (0,) * _n))
    n_heads = {"diff": 2 * DIFF_HEADS, "mla": MLA_HEADS, "gqa": GQA_HEADS}[group]
    k_width = MLA_HEAD_W if group == "mla" else q_list[0].shape[1]
    max_chunk = max(chunk_counts)
    scratch = [pltpu.VMEM((n_heads, 1, tq), F32),
               pltpu.VMEM((n_heads, 64 + ONES_ROWS, tq), F32),
               pltpu.VMEM((len(q_list), n_heads, tq, k_width), BF16)]
    scratch += [pltpu.VMEM((max_chunk, tq), F32)] * PARK_BUFFERS
    scratch += [pltpu.VMEM((1, tq), F32)] * PARK_BUFFERS
    return pl.pallas_call(
        functools.partial(_attn_kernel, group, len(q_list), tuple(chunk_counts), lam_init),
        grid=(n_bat, n_q),
        in_specs=in_specs,
        out_specs=pl.BlockSpec((tq, GROUP_W), lambda b, i: (b * n_q + i, 0)),
        out_shape=jax.ShapeDtypeStruct((n_bat * seq, GROUP_W), BF16),
        scratch_shapes=scratch,
        compiler_params=_params(2),
        name="attn_" + group,
    )(*inputs)


def _own_kv(k, vt, seq):
    if k.ndim == 3:
        kspec = pl.BlockSpec((k.shape[0], seq, k.shape[2]), lambda b, i: (0, b, 0))
    else:
        kspec = pl.BlockSpec((seq, k.shape[1]), lambda b, i: (b, 0))
    vspec = pl.BlockSpec((None,) + vt.shape[1:], lambda b, i: (b, 0, 0, 0))
    return (k, vt, kspec, vspec, vt.shape[3])


def _cache_kv(k, vt, layer):
    past = vt.shape[3]
    vt4 = vt.reshape(vt.shape[0], vt.shape[1], 1, vt.shape[2], past)
    kspec = pl.BlockSpec((None, None) + k.shape[2:], lambda b, i, _n=k.ndim - 2: (layer, b) + (0,) * _n)
    vspec = pl.BlockSpec((None, None, 1, vt.shape[2], past), lambda b, i: (layer, b, 0, 0, 0))
    return (k, vt4, kspec, vspec, past)


def _fourier_kernel(n_bat, cos_ref, sin_ref, ucs_ref, o_ref):
    cs = cos_ref[...]
    sn = sin_ref[...]
    for b in range(n_bat):
        u = ucs_ref[b]
        y = _dot(cs, u[:, :GROUP_W]) + _dot(sn, u[:, GROUP_W:])
        o_ref[b] = y.astype(o_ref.dtype)


def _fourier(ucs, n_bat, seq, cos_tab, sin_tab):
    tm = min(FOURIER_ROWS, seq)
    u3 = ucs.reshape(n_bat, seq, 2 * GROUP_W)
    out = pl.pallas_call(
        functools.partial(_fourier_kernel, n_bat),
        grid=(seq // tm,),
        in_specs=[pl.BlockSpec((tm, seq), lambda i: (i, 0)),
                  pl.BlockSpec((tm, seq), lambda i: (i, 0)),
                  pl.BlockSpec((n_bat, seq, 2 * GROUP_W), lambda i: (0, 0, 0))],
        out_specs=pl.BlockSpec((n_bat, tm, GROUP_W), lambda i: (0, i, 0)),
        out_shape=jax.ShapeDtypeStruct((n_bat, seq, GROUP_W), BF16),
        compiler_params=_params(1),
        name="fourier",
    )(cos_tab, sin_tab, u3)
    return out.reshape(n_bat * seq, GROUP_W)


def _post_kernel(final, x_ref, od_ref, yf_ref, om_ref, og_ref, mod_ref, wout_ref, gmlp_ref,
                 w1_ref, w2_ref, gfin_ref, o_ref):
    mod = mod_ref[...]
    gate1 = mod[:, 2 * D_MODEL:3 * D_MODEL]
    shift2 = mod[:, 3 * D_MODEL:4 * D_MODEL]
    scale2 = mod[:, 4 * D_MODEL:5 * D_MODEL]
    gate2 = mod[:, 5 * D_MODEL:6 * D_MODEL]
    mix = (_dot(od_ref[...], wout_ref[0]) + _dot(yf_ref[...], wout_ref[1])
           + _dot(om_ref[...], wout_ref[2]) + _dot(og_ref[...], wout_ref[3]))
    x1 = x_ref[...] + gate1 * mix
    h = (_rms_full(x1, gmlp_ref[...], D_MODEL) * (1.0 + scale2) + shift2).astype(BF16)
    acc = jnp.zeros(x1.shape, F32)
    for c in range(D_FF // MLP_CHUNK):
        sl = slice(c * MLP_CHUNK, (c + 1) * MLP_CHUNK)
        a = jnp.maximum(_dot(h, w1_ref[:, sl]), 0.0)
        acc = acc + _dot((a * a).astype(BF16), w2_ref[sl, :])
    x2 = x1 + gate2 * acc
    if final:
        x2 = _rms_full(x2, gfin_ref[...], D_MODEL)
    o_ref[...] = x2


def _post(x2d, o_diff, y_f, o_mla, o_gqa, mods3, mod_row_fn, lw, g_final, final, name):
    n_tok = x2d.shape[0]
    tb = TOKEN_BLOCK
    tok = lambda w: pl.BlockSpec((tb, w), lambda i: (i, 0))
    const = lambda a: pl.BlockSpec(a.shape, lambda i, _n=a.ndim: (0,) * _n)
    return pl.pallas_call(
        functools.partial(_post_kernel, final),
        grid=(n_tok // tb,),
        in_specs=[tok(D_MODEL), tok(GROUP_W), tok(GROUP_W), tok(GROUP_W), tok(GROUP_W),
                  pl.BlockSpec((None, 1, N_MOD * D_MODEL), lambda i: (mod_row_fn(i), 0, 0)),
                  const(lw["w_out"]), const(lw["gmlp"]), const(lw["w1"]), const(lw["w2"]), const(g_final)],
        out_specs=tok(D_MODEL),
        out_shape=jax.ShapeDtypeStruct((n_tok, D_MODEL), F32),
        compiler_params=_params(1),
        name=name,
    )(x2d, o_diff, y_f, o_mla, o_gqa, mods3, lw["w_out"], lw["gmlp"], lw["w1"], lw["w2"], g_final)


def _rope_tables(n_lat, head_dim, width, lane0=0):
    a = head_dim // 2
    inv = ROPE_THETA ** (-jnp.arange(0, a, 2, dtype=F32) / a)
    n_rows = n_lat // GRID_W
    lane = jnp.arange(width, dtype=jnp.int32) - lane0
    in_pat = (lane >= 0) & ((lane < head_dim) | (lane0 == 0))
    li = jnp.where(in_pat, lane % head_dim, 0)
    part = li // a
    ii = li % a
    first = (ii // (a // 2)) == 0
    f = ii % (a // 2)
    g = jnp.arange(max(n_rows, GRID_W), dtype=jnp.int32).astype(F32)
    ang = g[:, None] * inv[f][None, :]
    cos_g = jnp.where(in_pat[None, :], jnp.cos(ang), 1.0)
    ssin_g = jnp.where(in_pat[None, :], jnp.where(first[None, :], -jnp.sin(ang), jnp.sin(ang)), 0.0)

    def spread(tab):
        by_row = jnp.repeat(tab[:n_rows], GRID_W, axis=0)
        by_col = jnp.tile(tab[:GRID_W], (n_rows, 1))
        return jnp.where(part[None, :] == 0, by_row, by_col)

    return spread(cos_g), spread(ssin_g)


def _dft_tables(n):
    s = n ** -0.5
    k = jnp.arange(n, dtype=jnp.int32)

    def cos_sin(rows, scale_rows):
        r = jnp.arange(rows, dtype=jnp.int32) * scale_rows
        ang = ((r[:, None] * k[None, :]) % n).astype(F32) * (2.0 * math.pi / n)
        return jnp.cos(ang), jnp.sin(ang)

    if n <= 4 * GRID_W:
        c, sn = cos_sin(n, 1)
        return (c * s).astype(BF16), (sn * s).astype(BF16)
    n1 = GRID_W
    ca, sa = cos_sin(n // n1, n1)
    cb, sb = cos_sin(n1, 1)
    cos = ca[:, None, :] * cb[None, :, :] - sa[:, None, :] * sb[None, :, :]
    sin = sa[:, None, :] * cb[None, :, :] + ca[:, None, :] * sb[None, :, :]
    return (cos * s).astype(BF16).reshape(n, n), (sin * s).astype(BF16).reshape(n, n)


def _constants():
    eye = lambda n: jnp.eye(n, dtype=F32)
    ones_q = jnp.kron(eye(GQA_HEADS), jnp.ones((GQA_DIM, GQA_DIM), F32)).astype(BF16)
    ones_k = jnp.kron(eye(GQA_KV_HEADS), jnp.ones((GQA_DIM, GQA_DIM), F32)).astype(BF16)
    dup = jnp.kron(eye(GQA_KV_HEADS), jnp.concatenate([eye(GQA_DIM), eye(GQA_DIM)], axis=1)).astype(BF16)
    one_head = jnp.concatenate([jnp.zeros((MLA_ROPE, KR_LANE), F32), eye(MLA_ROPE),
                                jnp.zeros((MLA_ROPE, MLA_HEAD_W - KR_LANE - MLA_ROPE), F32)], axis=1)
    place_kr = jnp.tile(one_head, (1, MLA_HEADS)).astype(BF16)
    j = jnp.arange(FNET_CH, dtype=jnp.int32)
    ang = ((j[:, None] * j[None, :]) % FNET_CH).astype(F32) * (2.0 * math.pi / FNET_CH)
    cc = jnp.kron(eye(FNET_GROUPS), jnp.cos(ang) * FNET_CH ** -0.5)
    sc = jnp.kron(eye(FNET_GROUPS), jnp.sin(ang) * FNET_CH ** -0.5)
    csc = jnp.concatenate([cc, -sc], axis=1).astype(BF16)
    return dict(ones_q=ones_q, ones_k=ones_k, dup=dup, place_kr=place_kr, csc=csc)


def _layer_weights(l, norm_mix_g, norm_mlp_g, w_in, mla_q_norm_g, mla_w_uq, mla_kv_norm_g, mla_w_ukv,
                   gqa_q_norm_g, gqa_k_norm_g, w_out, mlp_w1, mlp_w2):
    wi = w_in[l]
    zc = lambda n: jnp.zeros((D_MODEL, n), F32)
    off = [0, 256, 512, 768, 1024, 1216, 1344, 1376, 1632, 1760, 1888]
    seg = [wi[:, off[k]:off[k + 1]] for k in range(10)]
    w_in_p = jnp.concatenate([
        seg[0], seg[1], seg[2], seg[3],
        seg[4], zc(256 - MLA_Q_RANK),
        seg[5],
        zc(KR_LANE), seg[6], zc(MLA_HEAD_W - KR_LANE - MLA_ROPE),
        seg[7], seg[8], seg[9]], axis=1).astype(BF16)
    wuq = mla_w_uq[l].reshape(MLA_Q_RANK, MLA_HEADS, MLA_NOPE + MLA_ROPE)
    wuq = jnp.pad(wuq, ((0, 256 - MLA_Q_RANK), (0, 0), (0, MLA_HEAD_W - MLA_NOPE - MLA_ROPE)))
    w_uq_p = wuq.reshape(256, MLA_HEADS * MLA_HEAD_W).astype(BF16)
    wukv = mla_w_ukv[l].reshape(MLA_KV_RANK, MLA_HEADS, MLA_NOPE + MLA_V)
    wk = jnp.pad(wukv[:, :, :MLA_NOPE], ((0, 0), (0, 0), (0, MLA_HEAD_W - MLA_NOPE)))
    wv = wukv[:, :, MLA_NOPE:]
    w_ukv_p = jnp.concatenate([wk.reshape(MLA_KV_RANK, -1), wv.reshape(MLA_KV_RANK, -1)], axis=1).astype(BF16)
    return dict(
        gmix=norm_mix_g[l][None, :], gmlp=norm_mlp_g[l][None, :], w_in=w_in_p,
        gq=jnp.pad(mla_q_norm_g[l], (0, 256 - MLA_Q_RANK))[None, :], w_uq=w_uq_p,
        gkv=mla_kv_norm_g[l][None, :], w_ukv=w_ukv_p,
        gqn=jnp.tile(gqa_q_norm_g[l], GQA_HEADS)[None, :], gkn=jnp.tile(gqa_k_norm_g[l], GQA_KV_HEADS)[None, :],
        w_out=w_out[l].reshape(4, GROUP_W, D_MODEL).astype(BF16),
        w1=mlp_w1[l].astype(BF16), w2=mlp_w2[l].astype(BF16))


def kernel(x_prompt, x_sample, cache_diff_k, cache_diff_v, cache_mla_ckv, cache_mla_krope, cache_gqa_k, cache_gqa_v, c, c_ctx, norm_mix_g, norm_mlp_g, ada_w, ada_b, w_in, diff_lambda, diff_subln_g, mla_q_norm_g, mla_w_uq, mla_kv_norm_g, mla_w_ukv, gqa_q_norm_g, gqa_k_norm_g, w_out, mlp_w1, mlp_w2, final_norm_g):
    n_ctx_b, seq, _ = x_prompt.shape
    n_lat_b, n_lat, _ = x_sample.shape
    depth = w_in.shape[0]
    assert seq % 128 == 0 and TOKEN_BLOCK % seq == 0 and (n_ctx_b * seq) % TOKEN_BLOCK == 0
    assert n_lat % TOKEN_BLOCK == 0 and n_lat % KEY_CHUNK == 0 and n_lat % GRID_W == 0

    consts = _constants()
    rows = 1 + n_lat_b
    rows_pad = -(-rows // 8) * 8
    cs = jnp.concatenate([c_ctx[None, :], c, jnp.zeros((rows_pad - rows, D_MODEL), F32)], axis=0)
    mods = _adaln(cs, ada_w, ada_b).reshape(depth * rows_pad, 1, N_MOD * D_MODEL)

    rope_tabs = (_rope_tables(n_lat, DIFF_QK, GROUP_W) + _rope_tables(n_lat, MLA_ROPE, MLA_HEAD_W, KR_LANE)
                 + _rope_tables(n_lat, GQA_DIM, GROUP_W))
    dft_ctx = _dft_tables(seq)
    dft_lat = _dft_tables(n_lat)

    layers = [_layer_weights(l, norm_mix_g, norm_mlp_g, w_in, mla_q_norm_g, mla_w_uq, mla_kv_norm_g,
                             mla_w_ukv, gqa_q_norm_g, gqa_k_norm_g, w_out, mlp_w1, mlp_w2)
              for l in range(depth)]
    cache = _cache_prep(cache_diff_k, cache_diff_v, cache_mla_ckv, cache_mla_krope, cache_gqa_k, cache_gqa_v,
                        jnp.stack([lw["w_ukv"] for lw in layers]), consts)

    g_final = final_norm_g[None, :]
    xp = x_prompt.reshape(n_ctx_b * seq, D_MODEL)
    xs = x_sample.reshape(n_lat_b * n_lat, D_MODEL)
    lat_blocks = n_lat // TOKEN_BLOCK
    states = [[] for _ in range(6)]
    for l in range(depth):
        lw = layers[l]
        lam_init = 0.8 - 0.6 * math.exp(-0.3 * l)
        ctx_row = lambda i, _l=l: _l * rows_pad
        lat_row = lambda i, _l=l: _l * rows_pad + 1 + i // lat_blocks
        lam_p = diff_lambda[l].astype(F32)
        final = l == depth - 1

        t = _pre(xp, mods, ctx_row, lw, consts, seq, latent=False)
        g_t = jnp.broadcast_to(diff_subln_g[l][:, None], (DIFF_V, min(Q_TILE, seq)))
        o_a = _attention("diff", [t["qd"]], [_own_kv(t["kd"], t["vdt"], seq)], n_ctx_b, seq, lam_init,
                         extras=(lam_p, g_t))
        o_c = _attention("mla", [t["qm"]], [_own_kv(t["km"], t["vmt"], seq)], n_ctx_b, seq)
        o_d = _attention("gqa", [t["qg"]], [_own_kv(t["kg"], t["vgt"], seq)], n_ctx_b, seq)
        y_f = _fourier(t["ucs"], n_ctx_b, seq, *dft_ctx)
        xp = _post(xp, o_a, y_f, o_c, o_d, mods, ctx_row, lw, g_final, final, "post_context")
        for k, nm in enumerate(["s_diff_k", "s_diff_v", "s_ckv", "s_kr", "s_gqa_k", "s_gqa_v"]):
            states[k].append(t[nm])

        t = _pre(xs, mods, lat_row, lw, consts, n_lat, latent=True, rope_tabs=rope_tabs)
        g_t = jnp.broadcast_to(diff_subln_g[l][:, None], (DIFF_V, min(Q_TILE, n_lat)))
        o_a = _attention("diff", [t["qd"], t["qd_rot"]],
                         [_cache_kv(cache["kd"], cache["vdt"], l), _own_kv(t["kd"], t["vdt"], n_lat)],
                         n_lat_b, n_lat, lam_init, extras=(lam_p, g_t))
        o_c = _attention("mla", [t["qm"], t["qm_rot"]],
                         [_cache_kv(cache["km"], cache["vmt"], l), _own_kv(t["km"], t["vmt"], n_lat)],
                         n_lat_b, n_lat)
        o_d = _attention("gqa", [t["qg"], t["qg_rot"]],
                         [_cache_kv(cache["kg"], cache["vgt"], l), _own_kv(t["kg"], t["vgt"], n_lat)],
                         n_lat_b, n_lat)
        y_f = _fourier(t["ucs"], n_lat_b, n_lat, *dft_lat)
        xs = _post(xs, o_a, y_f, o_c, o_d, mods, lat_row, lw, g_final, final, "post_latent")

    y_prompt = xp.reshape(n_ctx_b, seq, D_MODEL)
    y_sample = xs.reshape(n_lat_b, n_lat, D_MODEL)
    st = [jnp.stack([a.reshape(n_ctx_b, seq, -1) for a in s], axis=1) for s in states]
    new_diff_k = st[0].reshape(n_ctx_b, depth, seq, DIFF_HEADS, 2 * DIFF_QK)
    new_diff_v = st[1].reshape(n_ctx_b, depth, seq, DIFF_HEADS, DIFF_V)
    new_gqa_k = st[4].reshape(n_ctx_b, depth, seq, GQA_KV_HEADS, GQA_DIM)
    new_gqa_v = st[5].reshape(n_ctx_b, depth, seq, GQA_KV_HEADS, GQA_DIM)
    return (y_prompt, y_sample, new_diff_k, new_diff_v, st[2], st[3], new_gqa_k, new_gqa_v)
```

```python
import functools
import math

import jax
import jax.numpy as jnp
from jax import lax
from jax.experimental import pallas as pl
from jax.experimental.pallas import tpu as pltpu

F32 = jnp.float32
BF16 = jnp.bfloat16

D_MODEL = 1024
GRID_W = 64
GROUP_W = 256
DIFF_HEADS = 4
DIFF_QK = 32
DIFF_V = 64
FNET_GROUPS = 4
FNET_CH = 64
MLA_HEADS = 4
MLA_Q_RANK = 192
MLA_KV_RANK = 128
MLA_NOPE = 64
MLA_ROPE = 32
MLA_V = 64
MLA_HEAD_W = 128
GQA_HEADS = 4
GQA_KV_HEADS = 2
GQA_DIM = 64
D_FF = 4 * D_MODEL
N_MOD = 6
ROPE_THETA = 10000.0
EPS = 1e-6
LOG2E = math.log2(math.e)
DIFF_SCALE = DIFF_QK ** -0.5 * LOG2E
MLA_SCALE = (MLA_NOPE + MLA_ROPE) ** -0.5 * LOG2E
GQA_SCALE = GQA_DIM ** -0.5 * LOG2E
ONES_ROWS = 16
PARK_BUFFERS = 8

V7X_VMEM_LIMIT_BYTES = 56 * 1024 * 1024

C_AQ, C_AK, C_AV, C_BU, C_CQ, C_CKV, C_CKR, C_DQ, C_DK, C_DV, IN_W_PAD = (
    0, 256, 512, 768, 1024, 1280, 1408, 1536, 1792, 1920, 2048)
KR_LANE = MLA_NOPE

TOKEN_BLOCK = 512
KEY_CHUNK = 512
Q_TILE = {"diff": 512, "mla": 1024, "gqa": 512}
MLP_CHUNK = 1024
ADA_TILE = 1536
FOURIER_ROWS = 256


def _params(n_axes):
    return pltpu.CompilerParams(dimension_semantics=("arbitrary",) * n_axes,
                                vmem_limit_bytes=V7X_VMEM_LIMIT_BYTES)


def _dot(a, b):
    return jnp.dot(a, b, preferred_element_type=F32)


def _dot_nt(a, b):
    return lax.dot_general(a, b, (((1,), (1,)), ((), ())), preferred_element_type=F32)


def _adaln_kernel(c_ref, w_ref, b_ref, o_ref):
    c = c_ref[...]
    s = c / (1.0 + jnp.exp(-c))
    o_ref[...] = _dot(s.astype(BF16), w_ref[...].astype(BF16)) + b_ref[...]


def _adaln(cs, ada_w, ada_b):
    depth = ada_w.shape[0]
    rows = cs.shape[0]
    n_out = ada_w.shape[2]
    return pl.pallas_call(
        _adaln_kernel,
        grid=(depth, n_out // ADA_TILE),
        in_specs=[
            pl.BlockSpec((rows, D_MODEL), lambda l, j: (0, 0)),
            pl.BlockSpec((None, D_MODEL, ADA_TILE), lambda l, j: (l, 0, j)),
            pl.BlockSpec((None, 1, ADA_TILE), lambda l, j: (l, 0, j)),
        ],
        out_specs=pl.BlockSpec((None, rows, ADA_TILE), lambda l, j: (l, 0, j)),
        out_shape=jax.ShapeDtypeStruct((depth, rows, n_out), F32),
        compiler_params=_params(2),
        name="adaln",
    )(cs, ada_w, ada_b.reshape(depth, 1, n_out))


def _rms_full(x, g, n):
    ss = jnp.sum(x * x, axis=-1, keepdims=True)
    return x * lax.rsqrt(ss * (1.0 / n) + EPS) * g


def _group_sumsq(x, ones_bd):
    sq = x * x
    hi = sq.astype(BF16)
    lo = (sq - hi.astype(F32)).astype(BF16)
    return _dot(hi, ones_bd) + _dot(lo, ones_bd)


def _rope(x, cos, ssin, half):
    width = x.shape[-1]
    lane = lax.broadcasted_iota(jnp.int32, x.shape, 1)
    first = (lane & half) == 0
    partner = jnp.where(first, pltpu.roll(x, width - half, 1), pltpu.roll(x, half, 1))
    return x * cos + partner * ssin


def _pre_kernel(latent, n_sub, sub_rows, *refs):
    it = iter(refs)
    x_ref, mod_ref, gmix_ref, win_ref = next(it), next(it), next(it), next(it)
    gq_ref, wuq_ref, gkv_ref, wukv_ref = next(it), next(it), next(it), next(it)
    gqn_ref, gkn_ref, onesq_ref, onesk_ref, dup_ref, csc_ref = (
        next(it), next(it), next(it), next(it), next(it), next(it))
    if latent:
        cosd_ref, sind_ref, cosm_ref, sinm_ref, cosg_ref, sing_ref = (
            next(it), next(it), next(it), next(it), next(it), next(it))
    qd_ref, kd_ref, vdt_ref = next(it), next(it), next(it)
    qm_ref, km_ref, vmt_ref = next(it), next(it), next(it)
    qg_ref, kg_ref, vgt_ref = next(it), next(it), next(it)
    ucs_ref = next(it)
    if latent:
        qdr_ref, qmr_ref, qgr_ref = next(it), next(it), next(it)
    else:
        sdk_ref, sdv_ref, sckv_ref, skr_ref, sgk_ref, sgv_ref = (
            next(it), next(it), next(it), next(it), next(it), next(it))

    x = x_ref[...]
    mod = mod_ref[...]
    shift = mod[:, 0:D_MODEL]
    scale = mod[:, D_MODEL:2 * D_MODEL]
    h = _rms_full(x, gmix_ref[...], D_MODEL) * (1.0 + scale) + shift
    z = _dot(h.astype(BF16), win_ref[...])

    def store_t(ref, val):
        for s in range(n_sub):
            ref[s, 0] = val[s * sub_rows:(s + 1) * sub_rows, :].T.astype(BF16)

    a_q = z[:, C_AQ:C_AQ + GROUP_W]
    a_k = z[:, C_AK:C_AK + GROUP_W]
    a_v = z[:, C_AV:C_AV + GROUP_W]
    qd_ref[...] = (a_q * DIFF_SCALE).astype(BF16)
    if latent:
        cosd, sind = cosd_ref[...], sind_ref[...]
        qdr_ref[...] = (_rope(a_q, cosd, sind, DIFF_QK // 4) * DIFF_SCALE).astype(BF16)
        kd_ref[...] = _rope(a_k, cosd, sind, DIFF_QK // 4).astype(BF16)
    else:
        kd_ref[...] = a_k.astype(BF16)
        sdk_ref[...] = a_k
        sdv_ref[...] = a_v
    store_t(vdt_ref, a_v)

    ucs_ref[...] = _dot(z[:, C_BU:C_BU + GROUP_W].astype(BF16), csc_ref[...]).astype(BF16)

    cqn = _rms_full(z[:, C_CQ:C_CQ + 256], gq_ref[...], MLA_Q_RANK)
    q_m = _dot(cqn.astype(BF16), wuq_ref[...])
    qm_ref[...] = (q_m * MLA_SCALE).astype(BF16)
    ckv = _rms_full(z[:, C_CKV:C_CKV + MLA_KV_RANK], gkv_ref[...], MLA_KV_RANK)
    kv = _dot(ckv.astype(BF16), wukv_ref[...])
    kr = z[:, C_CKR:C_CKR + MLA_HEAD_W]
    if latent:
        cosm, sinm = cosm_ref[...], sinm_ref[...]
        for hd in range(MLA_HEADS):
            sl = slice(hd * MLA_HEAD_W, (hd + 1) * MLA_HEAD_W)
            qmr_ref[:, sl] = (_rope(q_m[:, sl], cosm, sinm, MLA_ROPE // 4) * MLA_SCALE).astype(BF16)
        kr = _rope(kr, cosm, sinm, MLA_ROPE // 4)
    else:
        sckv_ref[...] = ckv
        skr_ref[...] = kr[:, KR_LANE:KR_LANE + MLA_ROPE]
    for hd in range(MLA_HEADS):
        sl = slice(hd * MLA_HEAD_W, (hd + 1) * MLA_HEAD_W)
        km_ref[hd] = (kv[:, sl] + kr).astype(BF16)
    store_t(vmt_ref, kv[:, MLA_HEADS * MLA_HEAD_W:])

    d_q = z[:, C_DQ:C_DQ + GROUP_W]
    qn = d_q * lax.rsqrt(_group_sumsq(d_q, onesq_ref[...]) * (1.0 / GQA_DIM) + EPS) * gqn_ref[...]
    d_k = z[:, C_DK:C_DK + GQA_KV_HEADS * GQA_DIM]
    kn = d_k * lax.rsqrt(_group_sumsq(d_k, onesk_ref[...]) * (1.0 / GQA_DIM) + EPS) * gkn_ref[...]
    d_v = z[:, C_DV:C_DV + GQA_KV_HEADS * GQA_DIM]
    qg_ref[...] = (qn * GQA_SCALE).astype(BF16)
    if latent:
        cosg, sing = cosg_ref[...], sing_ref[...]
        qgr_ref[...] = (_rope(qn, cosg, sing, GQA_DIM // 4) * GQA_SCALE).astype(BF16)
        k_att = _rope(kn, cosg[:, :GQA_KV_HEADS * GQA_DIM], sing[:, :GQA_KV_HEADS * GQA_DIM], GQA_DIM // 4)
    else:
        sgk_ref[...] = kn
        sgv_ref[...] = d_v
        k_att = kn
    kg_ref[...] = _dot(k_att.astype(BF16), dup_ref[...]).astype(BF16)
    store_t(vgt_ref, d_v)


def _pre(x2d, mods3, mod_row_fn, lw, consts, seq, latent, rope_tabs=None):
    n_tok = x2d.shape[0]
    tb = TOKEN_BLOCK
    n_blk = n_tok // tb
    n_bat = n_tok // seq
    if seq >= tb:
        n_sub, sub_rows = 1, tb
        blk_per_seq = seq // tb
        vt_shape = lambda r: (n_bat, blk_per_seq, r, tb)
        vt_spec = lambda r: pl.BlockSpec((1, 1, r, tb), lambda i: (i // blk_per_seq, i % blk_per_seq, 0, 0))
    else:
        n_sub, sub_rows = tb // seq, seq
        blk_per_seq = 1
        vt_shape = lambda r: (n_bat, 1, r, seq)
        vt_spec = lambda r: pl.BlockSpec((n_sub, 1, r, seq), lambda i: (i, 0, 0, 0))

    def full(a):
        nd = a.ndim
        return pl.BlockSpec(a.shape, lambda i, _n=nd: (0,) * _n)

    tok = lambda w: pl.BlockSpec((tb, w), lambda i: (i, 0))
    tok_shape = lambda w, dt: jax.ShapeDtypeStruct((n_tok, w), dt)

    inputs = [x2d, mods3, lw["gmix"], lw["w_in"], lw["gq"], lw["w_uq"], lw["gkv"], lw["w_ukv"],
              lw["gqn"], lw["gkn"], consts["ones_q"], consts["ones_k"], consts["dup"], consts["csc"]]
    in_specs = [tok(D_MODEL),
                pl.BlockSpec((None, 1, N_MOD * D_MODEL), lambda i: (mod_row_fn(i), 0, 0))]
    in_specs += [full(a) for a in inputs[2:]]
    if latent:
        for t in rope_tabs:
            inputs.append(t)
            in_specs.append(pl.BlockSpec((tb, t.shape[1]), lambda i: (i % blk_per_seq, 0)))

    names = ["qd", "kd", "vdt", "qm", "km", "vmt", "qg", "kg", "vgt", "ucs"]
    km_shape = jax.ShapeDtypeStruct((MLA_HEADS, n_tok, MLA_HEAD_W), BF16)
    km_spec = pl.BlockSpec((MLA_HEADS, tb, MLA_HEAD_W), lambda i: (0, i, 0))
    out_shape = [tok_shape(256, BF16), tok_shape(256, BF16), jax.ShapeDtypeStruct(vt_shape(256), BF16),
                 tok_shape(512, BF16), km_shape, jax.ShapeDtypeStruct(vt_shape(256), BF16),
                 tok_shape(256, BF16), tok_shape(256, BF16), jax.ShapeDtypeStruct(vt_shape(128), BF16),
                 tok_shape(512, BF16)]
    out_specs = [tok(256), tok(256), vt_spec(256), tok(512), km_spec, vt_spec(256),
                 tok(256), tok(256), vt_spec(128), tok(512)]
    if latent:
        names += ["qd_rot", "qm_rot", "qg_rot"]
        out_shape += [tok_shape(256, BF16), tok_shape(512, BF16), tok_shape(256, BF16)]
        out_specs += [tok(256), tok(512), tok(256)]
    else:
        names += ["s_diff_k", "s_diff_v", "s_ckv", "s_kr", "s_gqa_k", "s_gqa_v"]
        widths = [256, 256, MLA_KV_RANK, MLA_ROPE, 128, 128]
        out_shape += [tok_shape(w, F32) for w in widths]
        out_specs += [tok(w) for w in widths]

    outs = pl.pallas_call(
        functools.partial(_pre_kernel, latent, n_sub, sub_rows),
        grid=(n_blk,),
        in_specs=in_specs,
        out_specs=out_specs,
        out_shape=out_shape,
        compiler_params=_params(1),
        name="pre_latent" if latent else "pre_context",
    )(*inputs)
    return dict(zip(names, outs))


def _cache_kernel(dk_ref, dv_ref, ckv_ref, kr_ref, gk_ref, gv_ref, wukv_ref, place_ref, dup_ref,
                  kd_ref, vdt_ref, km_ref, vmt_ref, kg_ref, vgt_ref):
    kd_ref[...] = dk_ref[...].astype(BF16)
    vdt_ref[...] = dv_ref[...].T.astype(BF16)
    kv = _dot(ckv_ref[...].astype(BF16), wukv_ref[...])
    kr4 = _dot(kr_ref[...].astype(BF16), place_ref[...])
    n_k = MLA_HEADS * MLA_HEAD_W
    for hd in range(MLA_HEADS):
        sl = slice(hd * MLA_HEAD_W, (hd + 1) * MLA_HEAD_W)
        km_ref[hd] = (kv[:, sl] + kr4[:, sl]).astype(BF16)
    vmt_ref[...] = kv[:, n_k:].T.astype(BF16)
    kg_ref[...] = _dot(gk_ref[...].astype(BF16), dup_ref[...]).astype(BF16)
    vgt_ref[...] = gv_ref[...].T.astype(BF16)


def _cache_prep(cache_diff_k, cache_diff_v, cache_mla_ckv, cache_mla_krope, cache_gqa_k, cache_gqa_v,
                w_ukv_all, consts):
    db, depth, past = cache_diff_k.shape[:3]
    ins = [cache_diff_k.reshape(db, depth, past, 256), cache_diff_v.reshape(db, depth, past, 256),
           cache_mla_ckv, cache_mla_krope, cache_gqa_k.reshape(db, depth, past, 128),
           cache_gqa_v.reshape(db, depth, past, 128)]
    cspec = lambda w: pl.BlockSpec((None, None, past, w), lambda l, b: (b, l, 0, 0))
    in_specs = [cspec(a.shape[-1]) for a in ins]
    in_specs += [pl.BlockSpec((None,) + w_ukv_all.shape[1:], lambda l, b: (l, 0, 0)),
                 pl.BlockSpec(consts["place_kr"].shape, lambda l, b: (0, 0)),
                 pl.BlockSpec(consts["dup"].shape, lambda l, b: (0, 0))]
    kspec = lambda w: pl.BlockSpec((None, None, past, w), lambda l, b: (l, b, 0, 0))
    vspec = lambda r: pl.BlockSpec((None, None, r, past), lambda l, b: (l, b, 0, 0))
    kshape = lambda w: jax.ShapeDtypeStruct((depth, db, past, w), BF16)
    vshape = lambda r: jax.ShapeDtypeStruct((depth, db, r, past), BF16)
    outs = pl.pallas_call(
        _cache_kernel,
        grid=(depth, db),
        in_specs=in_specs,
        out_specs=[kspec(256), vspec(256),
                   pl.BlockSpec((None, None, MLA_HEADS, past, MLA_HEAD_W), lambda l, b: (l, b, 0, 0, 0)),
                   vspec(256), kspec(256), vspec(128)],
        out_shape=[kshape(256), vshape(256),
                   jax.ShapeDtypeStruct((depth, db, MLA_HEADS, past, MLA_HEAD_W), BF16),
                   vshape(256), kshape(256), vshape(128)],
        compiler_params=_params(2),
        name="cache_prep",
    )(*ins, w_ukv_all, consts["place_kr"], consts["dup"])
    return dict(zip(["kd", "vdt", "km", "vmt", "kg", "vgt"], outs))


def _aligned_ds(start, size):
    if isinstance(start, int):
        return pl.ds(start, size)
    return pl.ds(pl.multiple_of(start, size), size)


def _attn_kernel(group, n_parts, chunk_counts, lam_init, *refs):
    it = iter(refs)
    q_refs = [next(it) for _ in range(n_parts)]
    kv_refs = [(next(it), next(it)) for _ in range(n_parts)]
    if group == "diff":
        lam_ref, g_ref = next(it), next(it)
    o_ref = next(it)
    m_sc, acc_sc, qe_sc = next(it), next(it), next(it)
    s_bufs = [next(it) for _ in range(PARK_BUFFERS)]
    mc_bufs = [next(it) for _ in range(PARK_BUFFERS)]

    n_heads = {"diff": 2 * DIFF_HEADS, "mla": MLA_HEADS, "gqa": GQA_HEADS}[group]
    head_shift = n_heads.bit_length() - 1
    tq = o_ref.shape[0]

    m_sc[...] = jnp.full(m_sc.shape, -jnp.inf, F32)
    acc_sc[...] = jnp.zeros(acc_sc.shape, F32)
    for part in range(n_parts):
        if group == "mla":
            for hd in range(n_heads):
                qe_sc[part, hd] = q_refs[part][:, hd * MLA_HEAD_W:(hd + 1) * MLA_HEAD_W]
        else:
            q = q_refs[part][...].astype(F32)
            lane = lax.broadcasted_iota(jnp.int32, q.shape, 1)
            head_w = q.shape[1] // n_heads
            for hd in range(n_heads):
                keep = (lane >= hd * head_w) & (lane < (hd + 1) * head_w)
                qe_sc[part, hd] = jnp.where(keep, q, 0.0).astype(BF16)

    def split(item):
        return item >> head_shift, item & (n_heads - 1)

    def scores(part, item, slot):
        k_ref = kv_refs[part][0]
        chunk = chunk_counts[part]
        c, hd = split(item)
        rows = _aligned_ds(c * chunk, chunk)
        k_blk = k_ref[hd, rows, :] if group == "mla" else k_ref[rows, :]
        s = _dot_nt(k_blk, qe_sc[part, hd])
        s_bufs[slot][0:chunk, :] = s
        mc_bufs[slot][...] = jnp.max(s, axis=0, keepdims=True)

    def accumulate(part, item, slot):
        vt_ref = kv_refs[part][1]
        chunk = chunk_counts[part]
        c, hd = split(item)
        v_row = (hd if group == "mla" else hd >> 1) * 64
        vt_blk = vt_ref[c, _aligned_ds(v_row, 64), :]
        lhs = jnp.concatenate([vt_blk, jnp.ones((ONES_ROWS, chunk), BF16)], axis=0)
        m_old = m_sc[hd]
        m_new = jnp.maximum(m_old, mc_bufs[slot][...])
        alpha = jnp.exp2(m_old - m_new)
        p = jnp.exp2(s_bufs[slot][0:chunk, :] - m_new)
        acc_sc[hd] = alpha * acc_sc[hd] + _dot(lhs, p.astype(BF16))
        m_sc[hd] = m_new

    grp = PARK_BUFFERS // 2

    def group_scores(part, g, bset):
        for k in range(grp):
            scores(part, grp * g + k, grp * bset + k)

    def group_accumulate(part, g, bset):
        for k in range(grp):
            accumulate(part, grp * g + k, grp * bset + k)

    def step(part_a, g_a, set_a, part_b, g_b, set_b):
        for k in range(grp):
            scores(part_a, grp * g_a + k, grp * set_a + k)
            accumulate(part_b, grp * g_b + k, grp * set_b + k)

    n_groups = [kv_refs[part][1].shape[0] * n_heads // grp for part in range(n_parts)]
    assert all(n * grp == kv_refs[part][1].shape[0] * n_heads for part, n in enumerate(n_groups))
    last = n_parts - 1
    inline = [(part, g) for part in range(last) for g in range(n_groups[part])]
    off = len(inline) % 2
    group_scores(*(inline[0] if inline else (last, 0)), 0)
    for idx, (part, g) in enumerate(inline):
        nxt = inline[idx + 1] if idx + 1 < len(inline) else (last, 0)
        step(*nxt, (idx + 1) % 2, part, g, idx % 2)

    def two_steps(j, carry):
        step(last, 2 * j + 1, 1 - off, last, 2 * j, off)
        step(last, 2 * j + 2, off, last, 2 * j + 1, 1 - off)
        return carry

    n_loop = (n_groups[last] - 1) // 2
    if n_loop > 1:
        lax.fori_loop(0, n_loop, two_steps, 0)
    else:
        for j in range(n_loop):
            two_steps(j, 0)
    for g in range(2 * n_loop, n_groups[last]):
        if g + 1 < n_groups[last]:
            step(last, g + 1, (off + g + 1) % 2, last, g, (off + g) % 2)
        else:
            group_accumulate(last, g, (off + g) % 2)

    outs = [acc_sc[hd, 0:64, :] / acc_sc[hd, 64:65, :] for hd in range(n_heads)]

    if group == "diff":
        lp = lam_ref[...]
        lam = (jnp.exp(jnp.sum(lp[0:1] * lp[1:2], axis=-1, keepdims=True))
               - jnp.exp(jnp.sum(lp[2:3] * lp[3:4], axis=-1, keepdims=True)) + lam_init)
        g = g_ref[...]
        comb = []
        for hd in range(DIFF_HEADS):
            y = outs[2 * hd] - lam * outs[2 * hd + 1]
            ms = jnp.sum(y * y, axis=0, keepdims=True) * (1.0 / DIFF_V)
            comb.append(y * lax.rsqrt(ms + EPS) * g * (1.0 - lam_init))
        outs = comb
    o_ref[...] = jnp.concatenate(outs, axis=0).T.astype(o_ref.dtype)


def _attention(group, q_list, kv_list, n_bat, seq, lam_init=0.0, extras=()):
    tq = min(Q_TILE[group], seq)
    n_q = seq // tq
    in_specs, inputs, chunk_counts = [], [], []
    for q in q_list:
        inputs.append(q)
        in_specs.append(pl.BlockSpec((tq, q.shape[1]), lambda b, i: (b * n_q + i, 0)))
    for k, vt, kspec, vspec, chunk in kv_list:
        inputs += [k, vt]
        in_specs += [kspec, vspec]
        chunk_counts.append(chunk)
    for e in extras:
        inputs.append(e)
        in_specs.append(pl.BlockSpec(e.shape, lambda b, i, _n=e.ndim: (0,) * _n))
    n_heads = {"diff": 2 * DIFF_HEADS, "mla": MLA_HEADS, "gqa": GQA_HEADS}[group]
    k_width = MLA_HEAD_W if group == "mla" else q_list[0].shape[1]
    max_chunk = max(chunk_counts)
    scratch = [pltpu.VMEM((n_heads, 1, tq), F32),
               pltpu.VMEM((n_heads, 64 + ONES_ROWS, tq), F32),
               pltpu.VMEM((len(q_list), n_heads, tq, k_width), BF16)]
    scratch += [pltpu.VMEM((max_chunk, tq), F32)] * PARK_BUFFERS
    scratch += [pltpu.VMEM((1, tq), F32)] * PARK_BUFFERS
    return pl.pallas_call(
        functools.partial(_attn_kernel, group, len(q_list), tuple(chunk_counts), lam_init),
        grid=(n_bat, n_q),
        in_specs=in_specs,
        out_specs=pl.BlockSpec((tq, GROUP_W), lambda b, i: (b * n_q + i, 0)),
        out_shape=jax.ShapeDtypeStruct((n_bat * seq, GROUP_W), BF16),
        scratch_shapes=scratch,
        compiler_params=_params(2),
        name="attn_" + group,
    )(*inputs)


def _own_kv(k, vt, seq):
    if k.ndim == 3:
        kspec = pl.BlockSpec((k.shape[0], seq, k.shape[2]), lambda b, i: (0, b, 0))
    else:
        kspec = pl.BlockSpec((seq, k.shape[1]), lambda b, i: (b, 0))
    vspec = pl.BlockSpec((None,) + vt.shape[1:], lambda b, i: (b, 0, 0, 0))
    return (k, vt, kspec, vspec, vt.shape[3])


def _cache_kv(k, vt, layer):
    past = vt.shape[3]
    vt4 = vt.reshape(vt.shape[0], vt.shape[1], 1, vt.shape[2], past)
    kspec = pl.BlockSpec((None, None) + k.shape[2:], lambda b, i, _n=k.ndim - 2: (layer, b) + (0,) * _n)
    vspec = pl.BlockSpec((None, None, 1, vt.shape[2], past), lambda b, i: (layer, b, 0, 0, 0))
    return (k, vt4, kspec, vspec, past)


def _fourier_kernel(n_bat, cos_ref, sin_ref, ucs_ref, o_ref):
    cs = cos_ref[...]
    sn = sin_ref[...]
    for b in range(n_bat):
        u = ucs_ref[b]
        y = _dot(cs, u[:, :GROUP_W]) + _dot(sn, u[:, GROUP_W:])
        o_ref[b] = y.astype(o_ref.dtype)


def _fourier(ucs, n_bat, seq, cos_tab, sin_tab):
    tm = min(FOURIER_ROWS, seq)
    u3 = ucs.reshape(n_bat, seq, 2 * GROUP_W)
    out = pl.pallas_call(
        functools.partial(_fourier_kernel, n_bat),
        grid=(seq // tm,),
        in_specs=[pl.BlockSpec((tm, seq), lambda i: (i, 0)),
                  pl.BlockSpec((tm, seq), lambda i: (i, 0)),
                  pl.BlockSpec((n_bat, seq, 2 * GROUP_W), lambda i: (0, 0, 0))],
        out_specs=pl.BlockSpec((n_bat, tm, GROUP_W), lambda i: (0, i, 0)),
        out_shape=jax.ShapeDtypeStruct((n_bat, seq, GROUP_W), BF16),
        compiler_params=_params(1),
        name="fourier",
    )(cos_tab, sin_tab, u3)
    return out.reshape(n_bat * seq, GROUP_W)


def _post_kernel(final, x_ref, od_ref, yf_ref, om_ref, og_ref, mod_ref, wout_ref, gmlp_ref,
                 w1_ref, w2_ref, gfin_ref, o_ref):
    mod = mod_ref[...]
    gate1 = mod[:, 2 * D_MODEL:3 * D_MODEL]
    shift2 = mod[:, 3 * D_MODEL:4 * D_MODEL]
    scale2 = mod[:, 4 * D_MODEL:5 * D_MODEL]
    gate2 = mod[:, 5 * D_MODEL:6 * D_MODEL]
    mix = (_dot(od_ref[...], wout_ref[0]) + _dot(yf_ref[...], wout_ref[1])
           + _dot(om_ref[...], wout_ref[2]) + _dot(og_ref[...], wout_ref[3]))
    x1 = x_ref[...] + gate1 * mix
    h = (_rms_full(x1, gmlp_ref[...], D_MODEL) * (1.0 + scale2) + shift2).astype(BF16)
    acc = jnp.zeros(x1.shape, F32)
    for c in range(D_FF // MLP_CHUNK):
        sl = slice(c * MLP_CHUNK, (c + 1) * MLP_CHUNK)
        a = jnp.maximum(_dot(h, w1_ref[:, sl]), 0.0)
        acc = acc + _dot((a * a).astype(BF16), w2_ref[sl, :])
    x2 = x1 + gate2 * acc
    if final:
        x2 = _rms_full(x2, gfin_ref[...], D_MODEL)
    o_ref[...] = x2


def _post(x2d, o_diff, y_f, o_mla, o_gqa, mods3, mod_row_fn, lw, g_final, final, name):
    n_tok = x2d.shape[0]
    tb = TOKEN_BLOCK
    tok = lambda w: pl.BlockSpec((tb, w), lambda i: (i, 0))
    const = lambda a: pl.BlockSpec(a.shape, lambda i, _n=a.ndim: (0,) * _n)
    return pl.pallas_call(
        functools.partial(_post_kernel, final),
        grid=(n_tok // tb,),
        in_specs=[tok(D_MODEL), tok(GROUP_W), tok(GROUP_W), tok(GROUP_W), tok(GROUP_W),
                  pl.BlockSpec((None, 1, N_MOD * D_MODEL), lambda i: (mod_row_fn(i), 0, 0)),
                  const(lw["w_out"]), const(lw["gmlp"]), const(lw["w1"]), const(lw["w2"]), const(g_final)],
        out_specs=tok(D_MODEL),
        out_shape=jax.ShapeDtypeStruct((n_tok, D_MODEL), F32),
        compiler_params=_params(1),
        name=name,
    )(x2d, o_diff, y_f, o_mla, o_gqa, mods3, lw["w_out"], lw["gmlp"], lw["w1"], lw["w2"], g_final)


def _rope_tables(n_lat, head_dim, width, lane0=0):
    a = head_dim // 2
    inv = ROPE_THETA ** (-jnp.arange(0, a, 2, dtype=F32) / a)
    n_rows = n_lat // GRID_W
    lane = jnp.arange(width, dtype=jnp.int32) - lane0
    in_pat = (lane >= 0) & ((lane < head_dim) | (lane0 == 0))
    li = jnp.where(in_pat, lane % head_dim, 0)
    part = li // a
    ii = li % a
    first = (ii // (a // 2)) == 0
    f = ii % (a // 2)
    g = jnp.arange(max(n_rows, GRID_W), dtype=jnp.int32).astype(F32)
    ang = g[:, None] * inv[f][None, :]
    cos_g = jnp.where(in_pat[None, :], jnp.cos(ang), 1.0)
    ssin_g = jnp.where(in_pat[None, :], jnp.where(first[None, :], -jnp.sin(ang), jnp.sin(ang)), 0.0)

    def spread(tab):
        by_row = jnp.repeat(tab[:n_rows], GRID_W, axis=0)
        by_col = jnp.tile(tab[:GRID_W], (n_rows, 1))
        return jnp.where(part[None, :] == 0, by_row, by_col)

    return spread(cos_g), spread(ssin_g)


def _dft_tables(n):
    s = n ** -0.5
    k = jnp.arange(n, dtype=jnp.int32)

    def cos_sin(rows, scale_rows):
        r = jnp.arange(rows, dtype=jnp.int32) * scale_rows
        ang = ((r[:, None] * k[None, :]) % n).astype(F32) * (2.0 * math.pi / n)
        return jnp.cos(ang), jnp.sin(ang)

    if n <= 4 * GRID_W:
        c, sn = cos_sin(n, 1)
        return (c * s).astype(BF16), (sn * s).astype(BF16)
    n1 = GRID_W
    ca, sa = cos_sin(n // n1, n1)
    cb, sb = cos_sin(n1, 1)
    cos = ca[:, None, :] * cb[None, :, :] - sa[:, None, :] * sb[None, :, :]
    sin = sa[:, None, :] * cb[None, :, :] + ca[:, None, :] * sb[None, :, :]
    return (cos * s).astype(BF16).reshape(n, n), (sin * s).astype(BF16).reshape(n, n)


def _constants():
    eye = lambda n: jnp.eye(n, dtype=F32)
    ones_q = jnp.kron(eye(GQA_HEADS), jnp.ones((GQA_DIM, GQA_DIM), F32)).astype(BF16)
    ones_k = jnp.kron(eye(GQA_KV_HEADS), jnp.ones((GQA_DIM, GQA_DIM), F32)).astype(BF16)
    dup = jnp.kron(eye(GQA_KV_HEADS), jnp.concatenate([eye(GQA_DIM), eye(GQA_DIM)], axis=1)).astype(BF16)
    one_head = jnp.concatenate([jnp.zeros((MLA_ROPE, KR_LANE), F32), eye(MLA_ROPE),
                                jnp.zeros((MLA_ROPE, MLA_HEAD_W - KR_LANE - MLA_ROPE), F32)], axis=1)
    place_kr = jnp.tile(one_head, (1, MLA_HEADS)).astype(BF16)
    j = jnp.arange(FNET_CH, dtype=jnp.int32)
    ang = ((j[:, None] * j[None, :]) % FNET_CH).astype(F32) * (2.0 * math.pi / FNET_CH)
    cc = jnp.kron(eye(FNET_GROUPS), jnp.cos(ang) * FNET_CH ** -0.5)
    sc = jnp.kron(eye(FNET_GROUPS), jnp.sin(ang) * FNET_CH ** -0.5)
    csc = jnp.concatenate([cc, -sc], axis=1).astype(BF16)
    return dict(ones_q=ones_q, ones_k=ones_k, dup=dup, place_kr=place_kr, csc=csc)


def _layer_weights(l, norm_mix_g, norm_mlp_g, w_in, mla_q_norm_g, mla_w_uq, mla_kv_norm_g, mla_w_ukv,
                   gqa_q_norm_g, gqa_k_norm_g, w_out, mlp_w1, mlp_w2):
    wi = w_in[l]
    zc = lambda n: jnp.zeros((D_MODEL, n), F32)
    off = [0, 256, 512, 768, 1024, 1216, 1344, 1376, 1632, 1760, 1888]
    seg = [wi[:, off[k]:off[k + 1]] for k in range(10)]
    w_in_p = jnp.concatenate([
        seg[0], seg[1], seg[2], seg[3],
        seg[4], zc(256 - MLA_Q_RANK),
        seg[5],
        zc(KR_LANE), seg[6], zc(MLA_HEAD_W - KR_LANE - MLA_ROPE),
        seg[7], seg[8], seg[9]], axis=1).astype(BF16)
    wuq = mla_w_uq[l].reshape(MLA_Q_RANK, MLA_HEADS, MLA_NOPE + MLA_ROPE)
    wuq = jnp.pad(wuq, ((0, 256 - MLA_Q_RANK), (0, 0), (0, MLA_HEAD_W - MLA_NOPE - MLA_ROPE)))
    w_uq_p = wuq.reshape(256, MLA_HEADS * MLA_HEAD_W).astype(BF16)
    wukv = mla_w_ukv[l].reshape(MLA_KV_RANK, MLA_HEADS, MLA_NOPE + MLA_V)
    wk = jnp.pad(wukv[:, :, :MLA_NOPE], ((0, 0), (0, 0), (0, MLA_HEAD_W - MLA_NOPE)))
    wv = wukv[:, :, MLA_NOPE:]
    w_ukv_p = jnp.concatenate([wk.reshape(MLA_KV_RANK, -1), wv.reshape(MLA_KV_RANK, -1)], axis=1).astype(BF16)
    return dict(
        gmix=norm_mix_g[l][None, :], gmlp=norm_mlp_g[l][None, :], w_in=w_in_p,
        gq=jnp.pad(mla_q_norm_g[l], (0, 256 - MLA_Q_RANK))[None, :], w_uq=w_uq_p,
        gkv=mla_kv_norm_g[l][None, :], w_ukv=w_ukv_p,
        gqn=jnp.tile(gqa_q_norm_g[l], GQA_HEADS)[None, :], gkn=jnp.tile(gqa_k_norm_g[l], GQA_KV_HEADS)[None, :],
        w_out=w_out[l].reshape(4, GROUP_W, D_MODEL).astype(BF16),
        w1=mlp_w1[l].astype(BF16), w2=mlp_w2[l].astype(BF16))


def kernel(x_prompt, x_sample, cache_diff_k, cache_diff_v, cache_mla_ckv, cache_mla_krope, cache_gqa_k, cache_gqa_v, c, c_ctx, norm_mix_g, norm_mlp_g, ada_w, ada_b, w_in, diff_lambda, diff_subln_g, mla_q_norm_g, mla_w_uq, mla_kv_norm_g, mla_w_ukv, gqa_q_norm_g, gqa_k_norm_g, w_out, mlp_w1, mlp_w2, final_norm_g):
    n_ctx_b, seq, _ = x_prompt.shape
    n_lat_b, n_lat, _ = x_sample.shape
    depth = w_in.shape[0]
    assert seq % 128 == 0 and TOKEN_BLOCK % seq == 0 and (n_ctx_b * seq) % TOKEN_BLOCK == 0
    assert n_lat % TOKEN_BLOCK == 0 and n_lat % KEY_CHUNK == 0 and n_lat % GRID_W == 0

    consts = _constants()
    rows = 1 + n_lat_b
    rows_pad = -(-rows // 8) * 8
    cs = jnp.concatenate([c_ctx[None, :], c, jnp.zeros((rows_pad - rows, D_MODEL), F32)], axis=0)
    mods = _adaln(cs, ada_w, ada_b).reshape(depth * rows_pad, 1, N_MOD * D_MODEL)

    rope_tabs = (_rope_tables(n_lat, DIFF_QK, GROUP_W) + _rope_tables(n_lat, MLA_ROPE, MLA_HEAD_W, KR_LANE)
                 + _rope_tables(n_lat, GQA_DIM, GROUP_W))
    dft_ctx = _dft_tables(seq)
    dft_lat = _dft_tables(n_lat)

    layers = [_layer_weights(l, norm_mix_g, norm_mlp_g, w_in, mla_q_norm_g, mla_w_uq, mla_kv_norm_g,
                             mla_w_ukv, gqa_q_norm_g, gqa_k_norm_g, w_out, mlp_w1, mlp_w2)
              for l in range(depth)]
    cache = _cache_prep(cache_diff_k, cache_diff_v, cache_mla_ckv, cache_mla_krope, cache_gqa_k, cache_gqa_v,
                        jnp.stack([lw["w_ukv"] for lw in layers]), consts)

    g_final = final_norm_g[None, :]
    xp = x_prompt.reshape(n_ctx_b * seq, D_MODEL)
    xs = x_sample.reshape(n_lat_b * n_lat, D_MODEL)
    lat_blocks = n_lat // TOKEN_BLOCK
    states = [[] for _ in range(6)]
    for l in range(depth):
        lw = layers[l]
        lam_init = 0.8 - 0.6 * math.exp(-0.3 * l)
        ctx_row = lambda i, _l=l: _l * rows_pad
        lat_row = lambda i, _l=l: _l * rows_pad + 1 + i // lat_blocks
        lam_p = diff_lambda[l].astype(F32)
        final = l == depth - 1

        t = _pre(xp, mods, ctx_row, lw, consts, seq, latent=False)
        g_t = jnp.broadcast_to(diff_subln_g[l][:, None], (DIFF_V, min(Q_TILE["diff"], seq)))
        o_a = _attention("diff", [t["qd"]], [_own_kv(t["kd"], t["vdt"], seq)], n_ctx_b, seq, lam_init,
                         extras=(lam_p, g_t))
        o_c = _attention("mla", [t["qm"]], [_own_kv(t["km"], t["vmt"], seq)], n_ctx_b, seq)
        o_d = _attention("gqa", [t["qg"]], [_own_kv(t["kg"], t["vgt"], seq)], n_ctx_b, seq)
        y_f = _fourier(t["ucs"], n_ctx_b, seq, *dft_ctx)
        xp = _post(xp, o_a, y_f, o_c, o_d, mods, ctx_row, lw, g_final, final, "post_context")
        for k, nm in enumerate(["s_diff_k", "s_diff_v", "s_ckv", "s_kr", "s_gqa_k", "s_gqa_v"]):
            states[k].append(t[nm])

        t = _pre(xs, mods, lat_row, lw, consts, n_lat, latent=True, rope_tabs=rope_tabs)
        g_t = jnp.broadcast_to(diff_subln_g[l][:, None], (DIFF_V, min(Q_TILE["diff"], n_lat)))
        o_a = _attention("diff", [t["qd"], t["qd_rot"]],
                         [_cache_kv(cache["kd"], cache["vdt"], l), _own_kv(t["kd"], t["vdt"], n_lat)],
                         n_lat_b, n_lat, lam_init, extras=(lam_p, g_t))
        o_c = _attention("mla", [t["qm"], t["qm_rot"]],
                         [_cache_kv(cache["km"], cache["vmt"], l), _own_kv(t["km"], t["vmt"], n_lat)],
                         n_lat_b, n_lat)
        o_d = _attention("gqa", [t["qg"], t["qg_rot"]],
                         [_cache_kv(cache["kg"], cache["vgt"], l), _own_kv(t["kg"], t["vgt"], n_lat)],
                         n_lat_b, n_lat)
        y_f = _fourier(t["ucs"], n_lat_b, n_lat, *dft_lat)
        xs = _post(xs, o_a, y_f, o_c, o_d, mods, lat_row, lw, g_final, final, "post_latent")

    y_prompt = xp.reshape(n_ctx_b, seq, D_MODEL)
    y_sample = xs.reshape(n_lat_b, n_lat, D_MODEL)
    st = [jnp.stack([a.reshape(n_ctx_b, seq, -1) for a in s], axis=1) for s in states]
    new_diff_k = st[0].reshape(n_ctx_b, depth, seq, DIFF_HEADS, 2 * DIFF_QK)
    new_diff_v = st[1].reshape(n_ctx_b, depth, seq, DIFF_HEADS, DIFF_V)
    new_gqa_k = st[4].reshape(n_ctx_b, depth, seq, GQA_KV_HEADS, GQA_DIM)
    new_gqa_v = st[5].reshape(n_ctx_b, depth, seq, GQA_KV_HEADS, GQA_DIM)
    return (y_prompt, y_sample, new_diff_k, new_diff_v, st[2], st[3], new_gqa_k, new_gqa_v)
```

```python
import functools
import math

import jax
import jax.numpy as jnp
from jax import lax
from jax.experimental import pallas as pl
from jax.experimental.pallas import tpu as pltpu

F32 = jnp.float32
BF16 = jnp.bfloat16

D_MODEL = 1024
GRID_W = 64
GROUP_W = 256
DIFF_HEADS = 4
DIFF_QK = 32
DIFF_V = 64
FNET_GROUPS = 4
FNET_CH = 64
MLA_HEADS = 4
MLA_Q_RANK = 192
MLA_KV_RANK = 128
MLA_NOPE = 64
MLA_ROPE = 32
MLA_V = 64
MLA_HEAD_W = 128
GQA_HEADS = 4
GQA_KV_HEADS = 2
GQA_DIM = 64
D_FF = 4 * D_MODEL
N_MOD = 6
ROPE_THETA = 10000.0
EPS = 1e-6
LOG2E = math.log2(math.e)
DIFF_SCALE = DIFF_QK ** -0.5 * LOG2E
MLA_SCALE = (MLA_NOPE + MLA_ROPE) ** -0.5 * LOG2E
GQA_SCALE = GQA_DIM ** -0.5 * LOG2E
ONES_ROWS = 16
PARK_BUFFERS = 8

V7X_VMEM_LIMIT_BYTES = 56 * 1024 * 1024

C_AQ, C_AK, C_AV, C_BU, C_CQ, C_CKV, C_CKR, C_DQ, C_DK, C_DV, IN_W_PAD = (
    0, 256, 512, 768, 1024, 1280, 1408, 1536, 1792, 1920, 2048)
KR_LANE = MLA_NOPE

TOKEN_BLOCK = 512
KEY_CHUNK = 512
Q_TILE = {"diff": 1024, "mla": 1024, "gqa": 1024}
MLP_CHUNK = 1024
ADA_TILE = 1536
FOURIER_ROWS = 256


def _params(n_axes):
    return pltpu.CompilerParams(dimension_semantics=("arbitrary",) * n_axes,
                                vmem_limit_bytes=V7X_VMEM_LIMIT_BYTES)


def _dot(a, b):
    return jnp.dot(a, b, preferred_element_type=F32)


def _dot_nt(a, b):
    return lax.dot_general(a, b, (((1,), (1,)), ((), ())), preferred_element_type=F32)


def _adaln_kernel(c_ref, w_ref, b_ref, o_ref):
    c = c_ref[...]
    s = c / (1.0 + jnp.exp(-c))
    o_ref[...] = _dot(s.astype(BF16), w_ref[...].astype(BF16)) + b_ref[...]


def _adaln(cs, ada_w, ada_b):
    depth = ada_w.shape[0]
    rows = cs.shape[0]
    n_out = ada_w.shape[2]
    return pl.pallas_call(
        _adaln_kernel,
        grid=(depth, n_out // ADA_TILE),
        in_specs=[
            pl.BlockSpec((rows, D_MODEL), lambda l, j: (0, 0)),
            pl.BlockSpec((None, D_MODEL, ADA_TILE), lambda l, j: (l, 0, j)),
            pl.BlockSpec((None, 1, ADA_TILE), lambda l, j: (l, 0, j)),
        ],
        out_specs=pl.BlockSpec((None, rows, ADA_TILE), lambda l, j: (l, 0, j)),
        out_shape=jax.ShapeDtypeStruct((depth, rows, n_out), F32),
        compiler_params=_params(2),
        name="adaln",
    )(cs, ada_w, ada_b.reshape(depth, 1, n_out))


def _rms_full(x, g, n):
    ss = jnp.sum(x * x, axis=-1, keepdims=True)
    return x * lax.rsqrt(ss * (1.0 / n) + EPS) * g


def _group_sumsq(x, ones_bd):
    sq = x * x
    hi = sq.astype(BF16)
    lo = (sq - hi.astype(F32)).astype(BF16)
    return _dot(hi, ones_bd) + _dot(lo, ones_bd)


def _rope(x, cos, ssin, half):
    width = x.shape[-1]
    lane = lax.broadcasted_iota(jnp.int32, x.shape, 1)
    first = (lane & half) == 0
    partner = jnp.where(first, pltpu.roll(x, width - half, 1), pltpu.roll(x, half, 1))
    return x * cos + partner * ssin


def _pre_kernel(latent, n_sub, sub_rows, *refs):
    it = iter(refs)
    x_ref, mod_ref, gmix_ref, win_ref = next(it), next(it), next(it), next(it)
    gq_ref, wuq_ref, gkv_ref, wukv_ref = next(it), next(it), next(it), next(it)
    gqn_ref, gkn_ref, onesq_ref, onesk_ref, dup_ref, csc_ref = (
        next(it), next(it), next(it), next(it), next(it), next(it))
    if latent:
        cosd_ref, sind_ref, cosm_ref, sinm_ref, cosg_ref, sing_ref = (
            next(it), next(it), next(it), next(it), next(it), next(it))
    qd_ref, kd_ref, vdt_ref = next(it), next(it), next(it)
    qm_ref, km_ref, vmt_ref = next(it), next(it), next(it)
    qg_ref, kg_ref, vgt_ref = next(it), next(it), next(it)
    ucs_ref = next(it)
    if latent:
        qdr_ref, qmr_ref, qgr_ref = next(it), next(it), next(it)
    else:
        sdk_ref, sdv_ref, sckv_ref, skr_ref, sgk_ref, sgv_ref = (
            next(it), next(it), next(it), next(it), next(it), next(it))

    x = x_ref[...]
    mod = mod_ref[...]
    shift = mod[:, 0:D_MODEL]
    scale = mod[:, D_MODEL:2 * D_MODEL]
    h = _rms_full(x, gmix_ref[...], D_MODEL) * (1.0 + scale) + shift
    z = _dot(h.astype(BF16), win_ref[...])

    def store_t(ref, val):
        for s in range(n_sub):
            ref[s, 0] = val[s * sub_rows:(s + 1) * sub_rows, :].T.astype(BF16)

    a_q = z[:, C_AQ:C_AQ + GROUP_W]
    a_k = z[:, C_AK:C_AK + GROUP_W]
    a_v = z[:, C_AV:C_AV + GROUP_W]
    qd_ref[...] = (a_q * DIFF_SCALE).astype(BF16)
    if latent:
        cosd, sind = cosd_ref[...], sind_ref[...]
        qdr_ref[...] = (_rope(a_q, cosd, sind, DIFF_QK // 4) * DIFF_SCALE).astype(BF16)
        kd_ref[...] = _rope(a_k, cosd, sind, DIFF_QK // 4).astype(BF16)
    else:
        kd_ref[...] = a_k.astype(BF16)
        sdk_ref[...] = a_k
        sdv_ref[...] = a_v
    store_t(vdt_ref, a_v)

    ucs_ref[...] = _dot(z[:, C_BU:C_BU + GROUP_W].astype(BF16), csc_ref[...]).astype(BF16)

    cqn = _rms_full(z[:, C_CQ:C_CQ + 256], gq_ref[...], MLA_Q_RANK)
    q_m = _dot(cqn.astype(BF16), wuq_ref[...])
    qm_ref[...] = (q_m * MLA_SCALE).astype(BF16)
    ckv = _rms_full(z[:, C_CKV:C_CKV + MLA_KV_RANK], gkv_ref[...], MLA_KV_RANK)
    kv = _dot(ckv.astype(BF16), wukv_ref[...])
    kr = z[:, C_CKR:C_CKR + MLA_HEAD_W]
    if latent:
        cosm, sinm = cosm_ref[...], sinm_ref[...]
        for hd in range(MLA_HEADS):
            sl = slice(hd * MLA_HEAD_W, (hd + 1) * MLA_HEAD_W)
            qmr_ref[:, sl] = (_rope(q_m[:, sl], cosm, sinm, MLA_ROPE // 4) * MLA_SCALE).astype(BF16)
        kr = _rope(kr, cosm, sinm, MLA_ROPE // 4)
    else:
        sckv_ref[...] = ckv
        skr_ref[...] = kr[:, KR_LANE:KR_LANE + MLA_ROPE]
    for hd in range(MLA_HEADS):
        sl = slice(hd * MLA_HEAD_W, (hd + 1) * MLA_HEAD_W)
        km_ref[hd] = (kv[:, sl] + kr).astype(BF16)
    store_t(vmt_ref, kv[:, MLA_HEADS * MLA_HEAD_W:])

    d_q = z[:, C_DQ:C_DQ + GROUP_W]
    qn = d_q * lax.rsqrt(_group_sumsq(d_q, onesq_ref[...]) * (1.0 / GQA_DIM) + EPS) * gqn_ref[...]
    d_k = z[:, C_DK:C_DK + GQA_KV_HEADS * GQA_DIM]
    kn = d_k * lax.rsqrt(_group_sumsq(d_k, onesk_ref[...]) * (1.0 / GQA_DIM) + EPS) * gkn_ref[...]
    d_v = z[:, C_DV:C_DV + GQA_KV_HEADS * GQA_DIM]
    qg_ref[...] = (qn * GQA_SCALE).astype(BF16)
    if latent:
        cosg, sing = cosg_ref[...], sing_ref[...]
        qgr_ref[...] = (_rope(qn, cosg, sing, GQA_DIM // 4) * GQA_SCALE).astype(BF16)
        k_att = _rope(kn, cosg[:, :GQA_KV_HEADS * GQA_DIM], sing[:, :GQA_KV_HEADS * GQA_DIM], GQA_DIM // 4)
    else:
        sgk_ref[...] = kn
        sgv_ref[...] = d_v
        k_att = kn
    kg_ref[...] = _dot(k_att.astype(BF16), dup_ref[...]).astype(BF16)
    store_t(vgt_ref, d_v)


def _pre(x2d, mods3, mod_row_fn, lw, consts, seq, latent, rope_tabs=None):
    n_tok = x2d.shape[0]
    tb = TOKEN_BLOCK
    n_blk = n_tok // tb
    n_bat = n_tok // seq
    if seq >= tb:
        n_sub, sub_rows = 1, tb
        blk_per_seq = seq // tb
        vt_shape = lambda r: (n_bat, blk_per_seq, r, tb)
        vt_spec = lambda r: pl.BlockSpec((1, 1, r, tb), lambda i: (i // blk_per_seq, i % blk_per_seq, 0, 0))
    else:
        n_sub, sub_rows = tb // seq, seq
        blk_per_seq = 1
        vt_shape = lambda r: (n_bat, 1, r, seq)
        vt_spec = lambda r: pl.BlockSpec((n_sub, 1, r, seq), lambda i: (i, 0, 0, 0))

    def full(a):
        nd = a.ndim
        return pl.BlockSpec(a.shape, lambda i, _n=nd: (0,) * _n)

    tok = lambda w: pl.BlockSpec((tb, w), lambda i: (i, 0))
    tok_shape = lambda w, dt: jax.ShapeDtypeStruct((n_tok, w), dt)

    inputs = [x2d, mods3, lw["gmix"], lw["w_in"], lw["gq"], lw["w_uq"], lw["gkv"], lw["w_ukv"],
              lw["gqn"], lw["gkn"], consts["ones_q"], consts["ones_k"], consts["dup"], consts["csc"]]
    in_specs = [tok(D_MODEL),
                pl.BlockSpec((None, 1, N_MOD * D_MODEL), lambda i: (mod_row_fn(i), 0, 0))]
    in_specs += [full(a) for a in inputs[2:]]
    if latent:
        for t in rope_tabs:
            inputs.append(t)
            in_specs.append(pl.BlockSpec((tb, t.shape[1]), lambda i: (i % blk_per_seq, 0)))

    names = ["qd", "kd", "vdt", "qm", "km", "vmt", "qg", "kg", "vgt", "ucs"]
    km_shape = jax.ShapeDtypeStruct((MLA_HEADS, n_tok, MLA_HEAD_W), BF16)
    km_spec = pl.BlockSpec((MLA_HEADS, tb, MLA_HEAD_W), lambda i: (0, i, 0))
    out_shape = [tok_shape(256, BF16), tok_shape(256, BF16), jax.ShapeDtypeStruct(vt_shape(256), BF16),
                 tok_shape(512, BF16), km_shape, jax.ShapeDtypeStruct(vt_shape(256), BF16),
                 tok_shape(256, BF16), tok_shape(256, BF16), jax.ShapeDtypeStruct(vt_shape(128), BF16),
                 tok_shape(512, BF16)]
    out_specs = [tok(256), tok(256), vt_spec(256), tok(512), km_spec, vt_spec(256),
                 tok(256), tok(256), vt_spec(128), tok(512)]
    if latent:
        names += ["qd_rot", "qm_rot", "qg_rot"]
        out_shape += [tok_shape(256, BF16), tok_shape(512, BF16), tok_shape(256, BF16)]
        out_specs += [tok(256), tok(512), tok(256)]
    else:
        names += ["s_diff_k", "s_diff_v", "s_ckv", "s_kr", "s_gqa_k", "s_gqa_v"]
        widths = [256, 256, MLA_KV_RANK, MLA_ROPE, 128, 128]
        out_shape += [tok_shape(w, F32) for w in widths]
        out_specs += [tok(w) for w in widths]

    outs = pl.pallas_call(
        functools.partial(_pre_kernel, latent, n_sub, sub_rows),
        grid=(n_blk,),
        in_specs=in_specs,
        out_specs=out_specs,
        out_shape=out_shape,
        compiler_params=_params(1),
        name="pre_latent" if latent else "pre_context",
    )(*inputs)
    return dict(zip(names, outs))


def _cache_kernel(dk_ref, dv_ref, ckv_ref, kr_ref, gk_ref, gv_ref, wukv_ref, place_ref, dup_ref,
                  kd_ref, vdt_ref, km_ref, vmt_ref, kg_ref, vgt_ref):
    kd_ref[...] = dk_ref[...].astype(BF16)
    vdt_ref[...] = dv_ref[...].T.astype(BF16)
    kv = _dot(ckv_ref[...].astype(BF16), wukv_ref[...])
    kr4 = _dot(kr_ref[...].astype(BF16), place_ref[...])
    n_k = MLA_HEADS * MLA_HEAD_W
    for hd in range(MLA_HEADS):
        sl = slice(hd * MLA_HEAD_W, (hd + 1) * MLA_HEAD_W)
        km_ref[hd] = (kv[:, sl] + kr4[:, sl]).astype(BF16)
    vmt_ref[...] = kv[:, n_k:].T.astype(BF16)
    kg_ref[...] = _dot(gk_ref[...].astype(BF16), dup_ref[...]).astype(BF16)
    vgt_ref[...] = gv_ref[...].T.astype(BF16)


def _cache_prep(cache_diff_k, cache_diff_v, cache_mla_ckv, cache_mla_krope, cache_gqa_k, cache_gqa_v,
                w_ukv_all, consts):
    db, depth, past = cache_diff_k.shape[:3]
    ins = [cache_diff_k.reshape(db, depth, past, 256), cache_diff_v.reshape(db, depth, past, 256),
           cache_mla_ckv, cache_mla_krope, cache_gqa_k.reshape(db, depth, past, 128),
           cache_gqa_v.reshape(db, depth, past, 128)]
    cspec = lambda w: pl.BlockSpec((None, None, past, w), lambda l, b: (b, l, 0, 0))
    in_specs = [cspec(a.shape[-1]) for a in ins]
    in_specs += [pl.BlockSpec((None,) + w_ukv_all.shape[1:], lambda l, b: (l, 0, 0)),
                 pl.BlockSpec(consts["place_kr"].shape, lambda l, b: (0, 0)),
                 pl.BlockSpec(consts["dup"].shape, lambda l, b: (0, 0))]
    kspec = lambda w: pl.BlockSpec((None, None, past, w), lambda l, b: (l, b, 0, 0))
    vspec = lambda r: pl.BlockSpec((None, None, r, past), lambda l, b: (l, b, 0, 0))
    kshape = lambda w: jax.ShapeDtypeStruct((depth, db, past, w), BF16)
    vshape = lambda r: jax.ShapeDtypeStruct((depth, db, r, past), BF16)
    outs = pl.pallas_call(
        _cache_kernel,
        grid=(depth, db),
        in_specs=in_specs,
        out_specs=[kspec(256), vspec(256),
                   pl.BlockSpec((None, None, MLA_HEADS, past, MLA_HEAD_W), lambda l, b: (l, b, 0, 0, 0)),
                   vspec(256), kspec(256), vspec(128)],
        out_shape=[kshape(256), vshape(256),
                   jax.ShapeDtypeStruct((depth, db, MLA_HEADS, past, MLA_HEAD_W), BF16),
                   vshape(256), kshape(256), vshape(128)],
        compiler_params=_params(2),
        name="cache_prep",
    )(*ins, w_ukv_all, consts["place_kr"], consts["dup"])
    return dict(zip(["kd", "vdt", "km", "vmt", "kg", "vgt"], outs))


def _aligned_ds(start, size):
    if isinstance(start, int):
        return pl.ds(start, size)
    return pl.ds(pl.multiple_of(start, size), size)


def _attn_kernel(group, n_parts, chunk_counts, lam_init, *refs):
    it = iter(refs)
    q_refs = [next(it) for _ in range(n_parts)]
    kv_refs = [(next(it), next(it)) for _ in range(n_parts)]
    if group == "diff":
        lam_ref, g_ref = next(it), next(it)
    o_ref = next(it)
    m_sc, acc_sc, qe_sc = next(it), next(it), next(it)
    s_bufs = [next(it) for _ in range(PARK_BUFFERS)]
    mc_bufs = [next(it) for _ in range(PARK_BUFFERS)]

    n_heads = {"diff": 2 * DIFF_HEADS, "mla": MLA_HEADS, "gqa": GQA_HEADS}[group]
    head_shift = n_heads.bit_length() - 1
    tq = o_ref.shape[0]

    m_sc[...] = jnp.full(m_sc.shape, -jnp.inf, F32)
    acc_sc[...] = jnp.zeros(acc_sc.shape, F32)
    for part in range(n_parts):
        if group == "mla":
            for hd in range(n_heads):
                qe_sc[part, hd] = q_refs[part][:, hd * MLA_HEAD_W:(hd + 1) * MLA_HEAD_W]
        else:
            q = q_refs[part][...].astype(F32)
            lane = lax.broadcasted_iota(jnp.int32, q.shape, 1)
            head_w = q.shape[1] // n_heads
            for hd in range(n_heads):
                keep = (lane >= hd * head_w) & (lane < (hd + 1) * head_w)
                qe_sc[part, hd] = jnp.where(keep, q, 0.0).astype(BF16)

    def split(item):
        return item >> head_shift, item & (n_heads - 1)

    def scores(part, item, slot):
        k_ref = kv_refs[part][0]
        chunk = chunk_counts[part]
        c, hd = split(item)
        rows = _aligned_ds(c * chunk, chunk)
        k_blk = k_ref[hd, rows, :] if group == "mla" else k_ref[rows, :]
        s = _dot_nt(k_blk, qe_sc[part, hd])
        s_bufs[slot][0:chunk, :] = s
        mc_bufs[slot][...] = jnp.max(s, axis=0, keepdims=True)

    def accumulate(part, item, slot):
        vt_ref = kv_refs[part][1]
        chunk = chunk_counts[part]
        c, hd = split(item)
        v_row = (hd if group == "mla" else hd >> 1) * 64
        vt_blk = vt_ref[c, _aligned_ds(v_row, 64), :]
        lhs = jnp.concatenate([vt_blk, jnp.ones((ONES_ROWS, chunk), BF16)], axis=0)
        m_old = m_sc[hd]
        m_new = jnp.maximum(m_old, mc_bufs[slot][...])
        alpha = jnp.exp2(m_old - m_new)
        p = jnp.exp2(s_bufs[slot][0:chunk, :] - m_new)
        acc_sc[hd] = alpha * acc_sc[hd] + _dot(lhs, p.astype(BF16))
        m_sc[hd] = m_new

    grp = PARK_BUFFERS // 2

    def group_scores(part, g, bset):
        for k in range(grp):
            scores(part, grp * g + k, grp * bset + k)

    def group_accumulate(part, g, bset):
        for k in range(grp):
            accumulate(part, grp * g + k, grp * bset + k)

    def step(part_a, g_a, set_a, part_b, g_b, set_b):
        group_scores(part_a, g_a, set_a)
        group_accumulate(part_b, g_b, set_b)

    n_groups = [kv_refs[part][1].shape[0] * n_heads // grp for part in range(n_parts)]
    assert all(n * grp == kv_refs[part][1].shape[0] * n_heads for part, n in enumerate(n_groups))
    last = n_parts - 1
    inline = [(part, g) for part in range(last) for g in range(n_groups[part])]
    off = len(inline) % 2
    group_scores(*(inline[0] if inline else (last, 0)), 0)
    for idx, (part, g) in enumerate(inline):
        nxt = inline[idx + 1] if idx + 1 < len(inline) else (last, 0)
        step(*nxt, (idx + 1) % 2, part, g, idx % 2)

    def two_steps(j, carry):
        step(last, 2 * j + 1, 1 - off, last, 2 * j, off)
        step(last, 2 * j + 2, off, last, 2 * j + 1, 1 - off)
        return carry

    n_loop = (n_groups[last] - 1) // 2
    if n_loop > 1:
        lax.fori_loop(0, n_loop, two_steps, 0)
    else:
        for j in range(n_loop):
            two_steps(j, 0)
    for g in range(2 * n_loop, n_groups[last]):
        if g + 1 < n_groups[last]:
            step(last, g + 1, (off + g + 1) % 2, last, g, (off + g) % 2)
        else:
            group_accumulate(last, g, (off + g) % 2)

    outs = [acc_sc[hd, 0:64, :] / acc_sc[hd, 64:65, :] for hd in range(n_heads)]

    if group == "diff":
        lp = lam_ref[...]
        lam = (jnp.exp(jnp.sum(lp[0:1] * lp[1:2], axis=-1, keepdims=True))
               - jnp.exp(jnp.sum(lp[2:3] * lp[3:4], axis=-1, keepdims=True)) + lam_init)
        g = g_ref[...]
        comb = []
        for hd in range(DIFF_HEADS):
            y = outs[2 * hd] - lam * outs[2 * hd + 1]
            ms = jnp.sum(y * y, axis=0, keepdims=True) * (1.0 / DIFF_V)
            comb.append(y * lax.rsqrt(ms + EPS) * g * (1.0 - lam_init))
        outs = comb
    o_ref[...] = jnp.concatenate(outs, axis=0).T.astype(o_ref.dtype)


def _attention(group, q_list, kv_list, n_bat, seq, lam_init=0.0, extras=()):
    tq = min(Q_TILE[group], seq)
    n_q = seq // tq
    in_specs, inputs, chunk_counts = [], [], []
    for q in q_list:
        inputs.append(q)
        in_specs.append(pl.BlockSpec((tq, q.shape[1]), lambda b, i: (b * n_q + i, 0)))
    for k, vt, kspec, vspec, chunk in kv_list:
        inputs += [k, vt]
        in_specs += [kspec, vspec]
        chunk_counts.append(chunk)
    for e in extras:
        inputs.append(e)
        in_specs.append(pl.BlockSpec(e.shape, lambda b, i, _n=e.ndim: (0,) * _n))
    n_heads = {"diff": 2 * DIFF_HEADS, "mla": MLA_HEADS, "gqa": GQA_HEADS}[group]
    k_width = MLA_HEAD_W if group == "mla" else q_list[0].shape[1]
    max_chunk = max(chunk_counts)
    scratch = [pltpu.VMEM((n_heads, 1, tq), F32),
               pltpu.VMEM((n_heads, 64 + ONES_ROWS, tq), F32),
               pltpu.VMEM((len(q_list), n_heads, tq, k_width), BF16)]
    scratch += [pltpu.VMEM((max_chunk, tq), F32)] * PARK_BUFFERS
    scratch += [pltpu.VMEM((1, tq), F32)] * PARK_BUFFERS
    return pl.pallas_call(
        functools.partial(_attn_kernel, group, len(q_list), tuple(chunk_counts), lam_init),
        grid=(n_bat, n_q),
        in_specs=in_specs,
        out_specs=pl.BlockSpec((tq, GROUP_W), lambda b, i: (b * n_q + i, 0)),
        out_shape=jax.ShapeDtypeStruct((n_bat * seq, GROUP_W), BF16),
        scratch_shapes=scratch,
        compiler_params=_params(2),
        name="attn_" + group,
    )(*inputs)


def _own_kv(k, vt, seq):
    if k.ndim == 3:
        kspec = pl.BlockSpec((k.shape[0], seq, k.shape[2]), lambda b, i: (0, b, 0))
    else:
        kspec = pl.BlockSpec((seq, k.shape[1]), lambda b, i: (b, 0))
    vspec = pl.BlockSpec((None,) + vt.shape[1:], lambda b, i: (b, 0, 0, 0))
    return (k, vt, kspec, vspec, vt.shape[3])


def _cache_kv(k, vt, layer):
    past = vt.shape[3]
    vt4 = vt.reshape(vt.shape[0], vt.shape[1], 1, vt.shape[2], past)
    kspec = pl.BlockSpec((None, None) + k.shape[2:], lambda b, i, _n=k.ndim - 2: (layer, b) + (0,) * _n)
    vspec = pl.BlockSpec((None, None, 1, vt.shape[2], past), lambda b, i: (layer, b, 0, 0, 0))
    return (k, vt4, kspec, vspec, past)


def _fourier_kernel(n_bat, cos_ref, sin_ref, ucs_ref, o_ref):
    cs = cos_ref[...]
    sn = sin_ref[...]
    for b in range(n_bat):
        u = ucs_ref[b]
        y = _dot(cs, u[:, :GROUP_W]) + _dot(sn, u[:, GROUP_W:])
        o_ref[b] = y.astype(o_ref.dtype)


def _fourier(ucs, n_bat, seq, cos_tab, sin_tab):
    tm = min(FOURIER_ROWS, seq)
    u3 = ucs.reshape(n_bat, seq, 2 * GROUP_W)
    out = pl.pallas_call(
        functools.partial(_fourier_kernel, n_bat),
        grid=(seq // tm,),
        in_specs=[pl.BlockSpec((tm, seq), lambda i: (i, 0)),
                  pl.BlockSpec((tm, seq), lambda i: (i, 0)),
                  pl.BlockSpec((n_bat, seq, 2 * GROUP_W), lambda i: (0, 0, 0))],
        out_specs=pl.BlockSpec((n_bat, tm, GROUP_W), lambda i: (0, i, 0)),
        out_shape=jax.ShapeDtypeStruct((n_bat, seq, GROUP_W), BF16),
        compiler_params=_params(1),
        name="fourier",
    )(cos_tab, sin_tab, u3)
    return out.reshape(n_bat * seq, GROUP_W)


def _post_kernel(final, x_ref, od_ref, yf_ref, om_ref, og_ref, mod_ref, wout_ref, gmlp_ref,
                 w1_ref, w2_ref, gfin_ref, o_ref):
    mod = mod_ref[...]
    gate1 = mod[:, 2 * D_MODEL:3 * D_MODEL]
    shift2 = mod[:, 3 * D_MODEL:4 * D_MODEL]
    scale2 = mod[:, 4 * D_MODEL:5 * D_MODEL]
    gate2 = mod[:, 5 * D_MODEL:6 * D_MODEL]
    mix = (_dot(od_ref[...], wout_ref[0]) + _dot(yf_ref[...], wout_ref[1])
           + _dot(om_ref[...], wout_ref[2]) + _dot(og_ref[...], wout_ref[3]))
    x1 = x_ref[...] + gate1 * mix
    h = (_rms_full(x1, gmlp_ref[...], D_MODEL) * (1.0 + scale2) + shift2).astype(BF16)
    acc = jnp.zeros(x1.shape, F32)
    for c in range(D_FF // MLP_CHUNK):
        sl = slice(c * MLP_CHUNK, (c + 1) * MLP_CHUNK)
        a = jnp.maximum(_dot(h, w1_ref[:, sl]), 0.0)
        acc = acc + _dot((a * a).astype(BF16), w2_ref[sl, :])
    x2 = x1 + gate2 * acc
    if final:
        x2 = _rms_full(x2, gfin_ref[...], D_MODEL)
    o_ref[...] = x2


def _post(x2d, o_diff, y_f, o_mla, o_gqa, mods3, mod_row_fn, lw, g_final, final, name):
    n_tok = x2d.shape[0]
    tb = TOKEN_BLOCK
    tok = lambda w: pl.BlockSpec((tb, w), lambda i: (i, 0))
    const = lambda a: pl.BlockSpec(a.shape, lambda i, _n=a.ndim: (0,) * _n)
    return pl.pallas_call(
        functools.partial(_post_kernel, final),
        grid=(n_tok // tb,),
        in_specs=[tok(D_MODEL), tok(GROUP_W), tok(GROUP_W), tok(GROUP_W), tok(GROUP_W),
                  pl.BlockSpec((None, 1, N_MOD * D_MODEL), lambda i: (mod_row_fn(i), 0, 0)),
                  const(lw["w_out"]), const(lw["gmlp"]), const(lw["w1"]), const(lw["w2"]), const(g_final)],
        out_specs=tok(D_MODEL),
        out_shape=jax.ShapeDtypeStruct((n_tok, D_MODEL), F32),
        compiler_params=_params(1),
        name=name,
    )(x2d, o_diff, y_f, o_mla, o_gqa, mods3, lw["w_out"], lw["gmlp"], lw["w1"], lw["w2"], g_final)


def _rope_tables(n_lat, head_dim, width, lane0=0):
    a = head_dim // 2
    inv = ROPE_THETA ** (-jnp.arange(0, a, 2, dtype=F32) / a)
    n_rows = n_lat // GRID_W
    lane = jnp.arange(width, dtype=jnp.int32) - lane0
    in_pat = (lane >= 0) & ((lane < head_dim) | (lane0 == 0))
    li = jnp.where(in_pat, lane % head_dim, 0)
    part = li // a
    ii = li % a
    first = (ii // (a // 2)) == 0
    f = ii % (a // 2)
    g = jnp.arange(max(n_rows, GRID_W), dtype=jnp.int32).astype(F32)
    ang = g[:, None] * inv[f][None, :]
    cos_g = jnp.where(in_pat[None, :], jnp.cos(ang), 1.0)
    ssin_g = jnp.where(in_pat[None, :], jnp.where(first[None, :], -jnp.sin(ang), jnp.sin(ang)), 0.0)

    def spread(tab):
        by_row = jnp.repeat(tab[:n_rows], GRID_W, axis=0)
        by_col = jnp.tile(tab[:GRID_W], (n_rows, 1))
        return jnp.where(part[None, :] == 0, by_row, by_col)

    return spread(cos_g), spread(ssin_g)


def _dft_tables(n):
    s = n ** -0.5
    k = jnp.arange(n, dtype=jnp.int32)

    def cos_sin(rows, scale_rows):
        r = jnp.arange(rows, dtype=jnp.int32) * scale_rows
        ang = ((r[:, None] * k[None, :]) % n).astype(F32) * (2.0 * math.pi / n)
        return jnp.cos(ang), jnp.sin(ang)

    if n <= 4 * GRID_W:
        c, sn = cos_sin(n, 1)
        return (c * s).astype(BF16), (sn * s).astype(BF16)
    n1 = GRID_W
    ca, sa = cos_sin(n // n1, n1)
    cb, sb = cos_sin(n1, 1)
    cos = ca[:, None, :] * cb[None, :, :] - sa[:, None, :] * sb[None, :, :]
    sin = sa[:, None, :] * cb[None, :, :] + ca[:, None, :] * sb[None, :, :]
    return (cos * s).astype(BF16).reshape(n, n), (sin * s).astype(BF16).reshape(n, n)


def _constants():
    eye = lambda n: jnp.eye(n, dtype=F32)
    ones_q = jnp.kron(eye(GQA_HEADS), jnp.ones((GQA_DIM, GQA_DIM), F32)).astype(BF16)
    ones_k = jnp.kron(eye(GQA_KV_HEADS), jnp.ones((GQA_DIM, GQA_DIM), F32)).astype(BF16)
    dup = jnp.kron(eye(GQA_KV_HEADS), jnp.concatenate([eye(GQA_DIM), eye(GQA_DIM)], axis=1)).astype(BF16)
    one_head = jnp.concatenate([jnp.zeros((MLA_ROPE, KR_LANE), F32), eye(MLA_ROPE),
                                jnp.zeros((MLA_ROPE, MLA_HEAD_W - KR_LANE - MLA_ROPE), F32)], axis=1)
    place_kr = jnp.tile(one_head, (1, MLA_HEADS)).astype(BF16)
    j = jnp.arange(FNET_CH, dtype=jnp.int32)
    ang = ((j[:, None] * j[None, :]) % FNET_CH).astype(F32) * (2.0 * math.pi / FNET_CH)
    cc = jnp.kron(eye(FNET_GROUPS), jnp.cos(ang) * FNET_CH ** -0.5)
    sc = jnp.kron(eye(FNET_GROUPS), jnp.sin(ang) * FNET_CH ** -0.5)
    csc = jnp.concatenate([cc, -sc], axis=1).astype(BF16)
    return dict(ones_q=ones_q, ones_k=ones_k, dup=dup, place_kr=place_kr, csc=csc)


def _layer_weights(l, norm_mix_g, norm_mlp_g, w_in, mla_q_norm_g, mla_w_uq, mla_kv_norm_g, mla_w_ukv,
                   gqa_q_norm_g, gqa_k_norm_g, w_out, mlp_w1, mlp_w2):
    wi = w_in[l]
    zc = lambda n: jnp.zeros((D_MODEL, n), F32)
    off = [0, 256, 512, 768, 1024, 1216, 1344, 1376, 1632, 1760, 1888]
    seg = [wi[:, off[k]:off[k + 1]] for k in range(10)]
    w_in_p = jnp.concatenate([
        seg[0], seg[1], seg[2], seg[3],
        seg[4], zc(256 - MLA_Q_RANK),
        seg[5],
        zc(KR_LANE), seg[6], zc(MLA_HEAD_W - KR_LANE - MLA_ROPE),
        seg[7], seg[8], seg[9]], axis=1).astype(BF16)
    wuq = mla_w_uq[l].reshape(MLA_Q_RANK, MLA_HEADS, MLA_NOPE + MLA_ROPE)
    wuq = jnp.pad(wuq, ((0, 256 - MLA_Q_RANK), (0, 0), (0, MLA_HEAD_W - MLA_NOPE - MLA_ROPE)))
    w_uq_p = wuq.reshape(256, MLA_HEADS * MLA_HEAD_W).astype(BF16)
    wukv = mla_w_ukv[l].reshape(MLA_KV_RANK, MLA_HEADS, MLA_NOPE + MLA_V)
    wk = jnp.pad(wukv[:, :, :MLA_NOPE], ((0, 0), (0, 0), (0, MLA_HEAD_W - MLA_NOPE)))
    wv = wukv[:, :, MLA_NOPE:]
    w_ukv_p = jnp.concatenate([wk.reshape(MLA_KV_RANK, -1), wv.reshape(MLA_KV_RANK, -1)], axis=1).astype(BF16)
    return dict(
        gmix=norm_mix_g[l][None, :], gmlp=norm_mlp_g[l][None, :], w_in=w_in_p,
        gq=jnp.pad(mla_q_norm_g[l], (0, 256 - MLA_Q_RANK))[None, :], w_uq=w_uq_p,
        gkv=mla_kv_norm_g[l][None, :], w_ukv=w_ukv_p,
        gqn=jnp.tile(gqa_q_norm_g[l], GQA_HEADS)[None, :], gkn=jnp.tile(gqa_k_norm_g[l], GQA_KV_HEADS)[None, :],
        w_out=w_out[l].reshape(4, GROUP_W, D_MODEL).astype(BF16),
        w1=mlp_w1[l].astype(BF16), w2=mlp_w2[l].astype(BF16))


def kernel(x_prompt, x_sample, cache_diff_k, cache_diff_v, cache_mla_ckv, cache_mla_krope, cache_gqa_k, cache_gqa_v, c, c_ctx, norm_mix_g, norm_mlp_g, ada_w, ada_b, w_in, diff_lambda, diff_subln_g, mla_q_norm_g, mla_w_uq, mla_kv_norm_g, mla_w_ukv, gqa_q_norm_g, gqa_k_norm_g, w_out, mlp_w1, mlp_w2, final_norm_g):
    n_ctx_b, seq, _ = x_prompt.shape
    n_lat_b, n_lat, _ = x_sample.shape
    depth = w_in.shape[0]
    assert seq % 128 == 0 and TOKEN_BLOCK % seq == 0 and (n_ctx_b * seq) % TOKEN_BLOCK == 0
    assert n_lat % TOKEN_BLOCK == 0 and n_lat % KEY_CHUNK == 0 and n_lat % GRID_W == 0

    consts = _constants()
    rows = 1 + n_lat_b
    rows_pad = -(-rows // 8) * 8
    cs = jnp.concatenate([c_ctx[None, :], c, jnp.zeros((rows_pad - rows, D_MODEL), F32)], axis=0)
    mods = _adaln(cs, ada_w, ada_b).reshape(depth * rows_pad, 1, N_MOD * D_MODEL)

    rope_tabs = (_rope_tables(n_lat, DIFF_QK, GROUP_W) + _rope_tables(n_lat, MLA_ROPE, MLA_HEAD_W, KR_LANE)
                 + _rope_tables(n_lat, GQA_DIM, GROUP_W))
    dft_ctx = _dft_tables(seq)
    dft_lat = _dft_tables(n_lat)

    layers = [_layer_weights(l, norm_mix_g, norm_mlp_g, w_in, mla_q_norm_g, mla_w_uq, mla_kv_norm_g,
                             mla_w_ukv, gqa_q_norm_g, gqa_k_norm_g, w_out, mlp_w1, mlp_w2)
              for l in range(depth)]
    cache = _cache_prep(cache_diff_k, cache_diff_v, cache_mla_ckv, cache_mla_krope, cache_gqa_k, cache_gqa_v,
                        jnp.stack([lw["w_ukv"] for lw in layers]), consts)

    g_final = final_norm_g[None, :]
    xp = x_prompt.reshape(n_ctx_b * seq, D_MODEL)
    xs = x_sample.reshape(n_lat_b * n_lat, D_MODEL)
    lat_blocks = n_lat // TOKEN_BLOCK
    states = [[] for _ in range(6)]
    for l in range(depth):
        lw = layers[l]
        lam_init = 0.8 - 0.6 * math.exp(-0.3 * l)
        ctx_row = lambda i, _l=l: _l * rows_pad
        lat_row = lambda i, _l=l: _l * rows_pad + 1 + i // lat_blocks
        lam_p = diff_lambda[l].astype(F32)
        final = l == depth - 1

        t = _pre(xp, mods, ctx_row, lw, consts, seq, latent=False)
        g_t = jnp.broadcast_to(diff_subln_g[l][:, None], (DIFF_V, min(Q_TILE["diff"], seq)))
        o_a = _attention("diff", [t["qd"]], [_own_kv(t["kd"], t["vdt"], seq)], n_ctx_b, seq, lam_init,
                         extras=(lam_p, g_t))
        o_c = _attention("mla", [t["qm"]], [_own_kv(t["km"], t["vmt"], seq)], n_ctx_b, seq)
        o_d = _attention("gqa", [t["qg"]], [_own_kv(t["kg"], t["vgt"], seq)], n_ctx_b, seq)
        y_f = _fourier(t["ucs"], n_ctx_b, seq, *dft_ctx)
        xp = _post(xp, o_a, y_f, o_c, o_d, mods, ctx_row, lw, g_final, final, "post_context")
        for k, nm in enumerate(["s_diff_k", "s_diff_v", "s_ckv", "s_kr", "s_gqa_k", "s_gqa_v"]):
            states[k].append(t[nm])

        t = _pre(xs, mods, lat_row, lw, consts, n_lat, latent=True, rope_tabs=rope_tabs)
        g_t = jnp.broadcast_to(diff_subln_g[l][:, None], (DIFF_V, min(Q_TILE["diff"], n_lat)))
        o_a = _attention("diff", [t["qd"], t["qd_rot"]],
                         [_cache_kv(cache["kd"], cache["vdt"], l), _own_kv(t["kd"], t["vdt"], n_lat)],
                         n_lat_b, n_lat, lam_init, extras=(lam_p, g_t))
        o_c = _attention("mla", [t["qm"], t["qm_rot"]],
                         [_cache_kv(cache["km"], cache["vmt"], l), _own_kv(t["km"], t["vmt"], n_lat)],
                         n_lat_b, n_lat)
        o_d = _attention("gqa", [t["qg"], t["qg_rot"]],
                         [_cache_kv(cache["kg"], cache["vgt"], l), _own_kv(t["kg"], t["vgt"], n_lat)],
                         n_lat_b, n_lat)
        y_f = _fourier(t["ucs"], n_lat_b, n_lat, *dft_lat)
        xs = _post(xs, o_a, y_f, o_c, o_d, mods, lat_row, lw, g_final, final, "post_latent")

    y_prompt = xp.reshape(n_ctx_b, seq, D_MODEL)
    y_sample = xs.reshape(n_lat_b, n_lat, D_MODEL)
    st = [jnp.stack([a.reshape(n_ctx_b, seq, -1) for a in s], axis=1) for s in states]
    new_diff_k = st[0].reshape(n_ctx_b, depth, seq, DIFF_HEADS, 2 * DIFF_QK)
    new_diff_v = st[1].reshape(n_ctx_b, depth, seq, DIFF_HEADS, DIFF_V)
    new_gqa_k = st[4].reshape(n_ctx_b, depth, seq, GQA_KV_HEADS, GQA_DIM)
    new_gqa_v = st[5].reshape(n_ctx_b, depth, seq, GQA_KV_HEADS, GQA_DIM)
    return (y_prompt, y_sample, new_diff_k, new_diff_v, st[2], st[3], new_gqa_k, new_gqa_v)
```

```python
import functools
import math

import jax
import jax.numpy as jnp
from jax import lax
from jax.experimental import pallas as pl
from jax.experimental.pallas import tpu as pltpu

F32 = jnp.float32
BF16 = jnp.bfloat16

D_MODEL = 1024
GRID_W = 64
GROUP_W = 256
DIFF_HEADS = 4
DIFF_QK = 32
DIFF_V = 64
FNET_GROUPS = 4
FNET_CH = 64
MLA_HEADS = 4
MLA_Q_RANK = 192
MLA_KV_RANK = 128
MLA_NOPE = 64
MLA_ROPE = 32
MLA_V = 64
MLA_HEAD_W = 128
GQA_HEADS = 4
GQA_KV_HEADS = 2
GQA_DIM = 64
D_FF = 4 * D_MODEL
N_MOD = 6
ROPE_THETA = 10000.0
EPS = 1e-6
LOG2E = math.log2(math.e)
DIFF_SCALE = DIFF_QK ** -0.5 * LOG2E
MLA_SCALE = (MLA_NOPE + MLA_ROPE) ** -0.5 * LOG2E
GQA_SCALE = GQA_DIM ** -0.5 * LOG2E
ONES_ROWS = 16
PARK_BUFFERS = 4

V7X_VMEM_LIMIT_BYTES = 56 * 1024 * 1024

C_AQ, C_AK, C_AV, C_BU, C_CQ, C_CKV, C_CKR, C_DQ, C_DK, C_DV, IN_W_PAD = (
    0, 256, 512, 768, 1024, 1280, 1408, 1536, 1792, 1920, 2048)
KR_LANE = MLA_NOPE

TOKEN_BLOCK = 512
KEY_CHUNK = 512
Q_TILE = {"diff": 1024, "mla": 1024, "gqa": 1024}
MLP_CHUNK = 1024
ADA_TILE = 1536
FOURIER_ROWS = 256


def _params(n_axes):
    return pltpu.CompilerParams(dimension_semantics=("arbitrary",) * n_axes,
                                vmem_limit_bytes=V7X_VMEM_LIMIT_BYTES)


def _dot(a, b):
    return jnp.dot(a, b, preferred_element_type=F32)


def _dot_nt(a, b):
    return lax.dot_general(a, b, (((1,), (1,)), ((), ())), preferred_element_type=F32)


def _adaln_kernel(c_ref, w_ref, b_ref, o_ref):
    c = c_ref[...]
    s = c / (1.0 + jnp.exp(-c))
    o_ref[...] = _dot(s.astype(BF16), w_ref[...].astype(BF16)) + b_ref[...]


def _adaln(cs, ada_w, ada_b):
    depth = ada_w.shape[0]
    rows = cs.shape[0]
    n_out = ada_w.shape[2]
    return pl.pallas_call(
        _adaln_kernel,
        grid=(depth, n_out // ADA_TILE),
        in_specs=[
            pl.BlockSpec((rows, D_MODEL), lambda l, j: (0, 0)),
            pl.BlockSpec((None, D_MODEL, ADA_TILE), lambda l, j: (l, 0, j)),
            pl.BlockSpec((None, 1, ADA_TILE), lambda l, j: (l, 0, j)),
        ],
        out_specs=pl.BlockSpec((None, rows, ADA_TILE), lambda l, j: (l, 0, j)),
        out_shape=jax.ShapeDtypeStruct((depth, rows, n_out), F32),
        compiler_params=_params(2),
        name="adaln",
    )(cs, ada_w, ada_b.reshape(depth, 1, n_out))


def _rms_full(x, g, n):
    ss = jnp.sum(x * x, axis=-1, keepdims=True)
    return x * lax.rsqrt(ss * (1.0 / n) + EPS) * g


def _group_sumsq(x, ones_bd):
    sq = x * x
    hi = sq.astype(BF16)
    lo = (sq - hi.astype(F32)).astype(BF16)
    return _dot(hi, ones_bd) + _dot(lo, ones_bd)


def _rope(x, cos, ssin, half):
    width = x.shape[-1]
    lane = lax.broadcasted_iota(jnp.int32, x.shape, 1)
    first = (lane & half) == 0
    partner = jnp.where(first, pltpu.roll(x, width - half, 1), pltpu.roll(x, half, 1))
    return x * cos + partner * ssin


def _pre_kernel(latent, n_sub, sub_rows, *refs):
    it = iter(refs)
    x_ref, mod_ref, gmix_ref, win_ref = next(it), next(it), next(it), next(it)
    gq_ref, wuq_ref, gkv_ref, wukv_ref = next(it), next(it), next(it), next(it)
    gqn_ref, gkn_ref, onesq_ref, onesk_ref, dup_ref, csc_ref = (
        next(it), next(it), next(it), next(it), next(it), next(it))
    if latent:
        cosd_ref, sind_ref, cosm_ref, sinm_ref, cosg_ref, sing_ref = (
            next(it), next(it), next(it), next(it), next(it), next(it))
    qd_ref, kd_ref, vdt_ref = next(it), next(it), next(it)
    qm_ref, km_ref, vmt_ref = next(it), next(it), next(it)
    qg_ref, kg_ref, vgt_ref = next(it), next(it), next(it)
    ucs_ref = next(it)
    if latent:
        qdr_ref, qmr_ref, qgr_ref = next(it), next(it), next(it)
    else:
        sdk_ref, sdv_ref, sckv_ref, skr_ref, sgk_ref, sgv_ref = (
            next(it), next(it), next(it), next(it), next(it), next(it))

    x = x_ref[...]
    mod = mod_ref[...]
    shift = mod[:, 0:D_MODEL]
    scale = mod[:, D_MODEL:2 * D_MODEL]
    h = _rms_full(x, gmix_ref[...], D_MODEL) * (1.0 + scale) + shift
    z = _dot(h.astype(BF16), win_ref[...])

    def store_t(ref, val):
        for s in range(n_sub):
            ref[s, 0] = val[s * sub_rows:(s + 1) * sub_rows, :].T.astype(BF16)

    a_q = z[:, C_AQ:C_AQ + GROUP_W]
    a_k = z[:, C_AK:C_AK + GROUP_W]
    a_v = z[:, C_AV:C_AV + GROUP_W]
    qd_ref[...] = (a_q * DIFF_SCALE).astype(BF16)
    if latent:
        cosd, sind = cosd_ref[...], sind_ref[...]
        qdr_ref[...] = (_rope(a_q, cosd, sind, DIFF_QK // 4) * DIFF_SCALE).astype(BF16)
        kd_ref[...] = _rope(a_k, cosd, sind, DIFF_QK // 4).astype(BF16)
    else:
        kd_ref[...] = a_k.astype(BF16)
        sdk_ref[...] = a_k
        sdv_ref[...] = a_v
    store_t(vdt_ref, a_v)

    ucs_ref[...] = _dot(z[:, C_BU:C_BU + GROUP_W].astype(BF16), csc_ref[...]).astype(BF16)

    cqn = _rms_full(z[:, C_CQ:C_CQ + 256], gq_ref[...], MLA_Q_RANK)
    q_m = _dot(cqn.astype(BF16), wuq_ref[...])
    qm_ref[...] = (q_m * MLA_SCALE).astype(BF16)
    ckv = _rms_full(z[:, C_CKV:C_CKV + MLA_KV_RANK], gkv_ref[...], MLA_KV_RANK)
    kv = _dot(ckv.astype(BF16), wukv_ref[...])
    kr = z[:, C_CKR:C_CKR + MLA_HEAD_W]
    if latent:
        cosm, sinm = cosm_ref[...], sinm_ref[...]
        for hd in range(MLA_HEADS):
            sl = slice(hd * MLA_HEAD_W, (hd + 1) * MLA_HEAD_W)
            qmr_ref[:, sl] = (_rope(q_m[:, sl], cosm, sinm, MLA_ROPE // 4) * MLA_SCALE).astype(BF16)
        kr = _rope(kr, cosm, sinm, MLA_ROPE // 4)
    else:
        sckv_ref[...] = ckv
        skr_ref[...] = kr[:, KR_LANE:KR_LANE + MLA_ROPE]
    for hd in range(MLA_HEADS):
        sl = slice(hd * MLA_HEAD_W, (hd + 1) * MLA_HEAD_W)
        km_ref[hd] = (kv[:, sl] + kr).astype(BF16)
    store_t(vmt_ref, kv[:, MLA_HEADS * MLA_HEAD_W:])

    d_q = z[:, C_DQ:C_DQ + GROUP_W]
    qn = d_q * lax.rsqrt(_group_sumsq(d_q, onesq_ref[...]) * (1.0 / GQA_DIM) + EPS) * gqn_ref[...]
    d_k = z[:, C_DK:C_DK + GQA_KV_HEADS * GQA_DIM]
    kn = d_k * lax.rsqrt(_group_sumsq(d_k, onesk_ref[...]) * (1.0 / GQA_DIM) + EPS) * gkn_ref[...]
    d_v = z[:, C_DV:C_DV + GQA_KV_HEADS * GQA_DIM]
    qg_ref[...] = (qn * GQA_SCALE).astype(BF16)
    if latent:
        cosg, sing = cosg_ref[...], sing_ref[...]
        qgr_ref[...] = (_rope(qn, cosg, sing, GQA_DIM // 4) * GQA_SCALE).astype(BF16)
        k_att = _rope(kn, cosg[:, :GQA_KV_HEADS * GQA_DIM], sing[:, :GQA_KV_HEADS * GQA_DIM], GQA_DIM // 4)
    else:
        sgk_ref[...] = kn
        sgv_ref[...] = d_v
        k_att = kn
    kg_ref[...] = _dot(k_att.astype(BF16), dup_ref[...]).astype(BF16)
    store_t(vgt_ref, d_v)


def _pre(x2d, mods3, mod_row_fn, lw, consts, seq, latent, rope_tabs=None):
    n_tok = x2d.shape[0]
    tb = TOKEN_BLOCK
    n_blk = n_tok // tb
    n_bat = n_tok // seq
    if seq >= tb:
        n_sub, sub_rows = 1, tb
        blk_per_seq = seq // tb
        vt_shape = lambda r: (n_bat, blk_per_seq, r, tb)
        vt_spec = lambda r: pl.BlockSpec((1, 1, r, tb), lambda i: (i // blk_per_seq, i % blk_per_seq, 0, 0))
    else:
        n_sub, sub_rows = tb // seq, seq
        blk_per_seq = 1
        vt_shape = lambda r: (n_bat, 1, r, seq)
        vt_spec = lambda r: pl.BlockSpec((n_sub, 1, r, seq), lambda i: (i, 0, 0, 0))

    def full(a):
        nd = a.ndim
        return pl.BlockSpec(a.shape, lambda i, _n=nd: (0,) * _n)

    tok = lambda w: pl.BlockSpec((tb, w), lambda i: (i, 0))
    tok_shape = lambda w, dt: jax.ShapeDtypeStruct((n_tok, w), dt)

    inputs = [x2d, mods3, lw["gmix"], lw["w_in"], lw["gq"], lw["w_uq"], lw["gkv"], lw["w_ukv"],
              lw["gqn"], lw["gkn"], consts["ones_q"], consts["ones_k"], consts["dup"], consts["csc"]]
    in_specs = [tok(D_MODEL),
                pl.BlockSpec((None, 1, N_MOD * D_MODEL), lambda i: (mod_row_fn(i), 0, 0))]
    in_specs += [full(a) for a in inputs[2:]]
    if latent:
        for t in rope_tabs:
            inputs.append(t)
            in_specs.append(pl.BlockSpec((tb, t.shape[1]), lambda i: (i % blk_per_seq, 0)))

    names = ["qd", "kd", "vdt", "qm", "km", "vmt", "qg", "kg", "vgt", "ucs"]
    km_shape = jax.ShapeDtypeStruct((MLA_HEADS, n_tok, MLA_HEAD_W), BF16)
    km_spec = pl.BlockSpec((MLA_HEADS, tb, MLA_HEAD_W), lambda i: (0, i, 0))
    out_shape = [tok_shape(256, BF16), tok_shape(256, BF16), jax.ShapeDtypeStruct(vt_shape(256), BF16),
                 tok_shape(512, BF16), km_shape, jax.ShapeDtypeStruct(vt_shape(256), BF16),
                 tok_shape(256, BF16), tok_shape(256, BF16), jax.ShapeDtypeStruct(vt_shape(128), BF16),
                 tok_shape(512, BF16)]
    out_specs = [tok(256), tok(256), vt_spec(256), tok(512), km_spec, vt_spec(256),
                 tok(256), tok(256), vt_spec(128), tok(512)]
    if latent:
        names += ["qd_rot", "qm_rot", "qg_rot"]
        out_shape += [tok_shape(256, BF16), tok_shape(512, BF16), tok_shape(256, BF16)]
        out_specs += [tok(256), tok(512), tok(256)]
    else:
        names += ["s_diff_k", "s_diff_v", "s_ckv", "s_kr", "s_gqa_k", "s_gqa_v"]
        widths = [256, 256, MLA_KV_RANK, MLA_ROPE, 128, 128]
        out_shape += [tok_shape(w, F32) for w in widths]
        out_specs += [tok(w) for w in widths]

    outs = pl.pallas_call(
        functools.partial(_pre_kernel, latent, n_sub, sub_rows),
        grid=(n_blk,),
        in_specs=in_specs,
        out_specs=out_specs,
        out_shape=out_shape,
        compiler_params=_params(1),
        name="pre_latent" if latent else "pre_context",
    )(*inputs)
    return dict(zip(names, outs))


def _cache_kernel(dk_ref, dv_ref, ckv_ref, kr_ref, gk_ref, gv_ref, wukv_ref, place_ref, dup_ref,
                  kd_ref, vdt_ref, km_ref, vmt_ref, kg_ref, vgt_ref):
    kd_ref[...] = dk_ref[...].astype(BF16)
    vdt_ref[...] = dv_ref[...].T.astype(BF16)
    kv = _dot(ckv_ref[...].astype(BF16), wukv_ref[...])
    kr4 = _dot(kr_ref[...].astype(BF16), place_ref[...])
    n_k = MLA_HEADS * MLA_HEAD_W
    for hd in range(MLA_HEADS):
        sl = slice(hd * MLA_HEAD_W, (hd + 1) * MLA_HEAD_W)
        km_ref[hd] = (kv[:, sl] + kr4[:, sl]).astype(BF16)
    vmt_ref[...] = kv[:, n_k:].T.astype(BF16)
    kg_ref[...] = _dot(gk_ref[...].astype(BF16), dup_ref[...]).astype(BF16)
    vgt_ref[...] = gv_ref[...].T.astype(BF16)


def _cache_prep(cache_diff_k, cache_diff_v, cache_mla_ckv, cache_mla_krope, cache_gqa_k, cache_gqa_v,
                w_ukv_all, consts):
    db, depth, past = cache_diff_k.shape[:3]
    ins = [cache_diff_k.reshape(db, depth, past, 256), cache_diff_v.reshape(db, depth, past, 256),
           cache_mla_ckv, cache_mla_krope, cache_gqa_k.reshape(db, depth, past, 128),
           cache_gqa_v.reshape(db, depth, past, 128)]
    cspec = lambda w: pl.BlockSpec((None, None, past, w), lambda l, b: (b, l, 0, 0))
    in_specs = [cspec(a.shape[-1]) for a in ins]
    in_specs += [pl.BlockSpec((None,) + w_ukv_all.shape[1:], lambda l, b: (l, 0, 0)),
                 pl.BlockSpec(consts["place_kr"].shape, lambda l, b: (0, 0)),
                 pl.BlockSpec(consts["dup"].shape, lambda l, b: (0, 0))]
    kspec = lambda w: pl.BlockSpec((None, None, past, w), lambda l, b: (l, b, 0, 0))
    vspec = lambda r: pl.BlockSpec((None, None, r, past), lambda l, b: (l, b, 0, 0))
    kshape = lambda w: jax.ShapeDtypeStruct((depth, db, past, w), BF16)
    vshape = lambda r: jax.ShapeDtypeStruct((depth, db, r, past), BF16)
    outs = pl.pallas_call(
        _cache_kernel,
        grid=(depth, db),
        in_specs=in_specs,
        out_specs=[kspec(256), vspec(256),
                   pl.BlockSpec((None, None, MLA_HEADS, past, MLA_HEAD_W), lambda l, b: (l, b, 0, 0, 0)),
                   vspec(256), kspec(256), vspec(128)],
        out_shape=[kshape(256), vshape(256),
                   jax.ShapeDtypeStruct((depth, db, MLA_HEADS, past, MLA_HEAD_W), BF16),
                   vshape(256), kshape(256), vshape(128)],
        compiler_params=_params(2),
        name="cache_prep",
    )(*ins, w_ukv_all, consts["place_kr"], consts["dup"])
    return dict(zip(["kd", "vdt", "km", "vmt", "kg", "vgt"], outs))


def _aligned_ds(start, size):
    if isinstance(start, int):
        return pl.ds(start, size)
    return pl.ds(pl.multiple_of(start, size), size)


def _attn_kernel(group, n_parts, chunk_counts, lam_init, *refs):
    it = iter(refs)
    q_refs = [next(it) for _ in range(n_parts)]
    kv_refs = [(next(it), next(it)) for _ in range(n_parts)]
    if group == "diff":
        lam_ref, g_ref = next(it), next(it)
    o_ref = next(it)
    m_sc, acc_sc, qe_sc = next(it), next(it), next(it)
    s_bufs = [next(it) for _ in range(PARK_BUFFERS)]
    mc_bufs = [next(it) for _ in range(PARK_BUFFERS)]

    n_heads = {"diff": 2 * DIFF_HEADS, "mla": MLA_HEADS, "gqa": GQA_HEADS}[group]
    head_shift = n_heads.bit_length() - 1
    tq = o_ref.shape[0]

    m_sc[...] = jnp.full(m_sc.shape, -jnp.inf, F32)
    acc_sc[...] = jnp.zeros(acc_sc.shape, F32)
    for part in range(n_parts):
        if group == "mla":
            for hd in range(n_heads):
                qe_sc[part, hd] = q_refs[part][:, hd * MLA_HEAD_W:(hd + 1) * MLA_HEAD_W]
        else:
            q = q_refs[part][...].astype(F32)
            lane = lax.broadcasted_iota(jnp.int32, q.shape, 1)
            head_w = q.shape[1] // n_heads
            for hd in range(n_heads):
                keep = (lane >= hd * head_w) & (lane < (hd + 1) * head_w)
                qe_sc[part, hd] = jnp.where(keep, q, 0.0).astype(BF16)

    def split(item):
        return item >> head_shift, item & (n_heads - 1)

    def scores(part, item, slot):
        k_ref = kv_refs[part][0]
        chunk = chunk_counts[part]
        c, hd = split(item)
        rows = _aligned_ds(c * chunk, chunk)
        k_blk = k_ref[hd, rows, :] if group == "mla" else k_ref[rows, :]
        s = _dot_nt(k_blk, qe_sc[part, hd])
        s_bufs[slot][0:chunk, :] = s
        mc_bufs[slot][...] = jnp.max(s, axis=0, keepdims=True)

    def accumulate(part, item, slot):
        vt_ref = kv_refs[part][1]
        chunk = chunk_counts[part]
        c, hd = split(item)
        v_row = (hd if group == "mla" else hd >> 1) * 64
        vt_blk = vt_ref[c, _aligned_ds(v_row, 64), :]
        lhs = jnp.concatenate([vt_blk, jnp.ones((ONES_ROWS, chunk), BF16)], axis=0)
        m_old = m_sc[hd]
        m_new = jnp.maximum(m_old, mc_bufs[slot][...])
        alpha = jnp.exp2(m_old - m_new)
        p = jnp.exp2(s_bufs[slot][0:chunk, :] - m_new)
        acc_sc[hd] = alpha * acc_sc[hd] + _dot(lhs, p.astype(BF16))
        m_sc[hd] = m_new

    grp = PARK_BUFFERS // 2

    def group_scores(part, g, bset):
        for k in range(grp):
            scores(part, grp * g + k, grp * bset + k)

    def group_accumulate(part, g, bset):
        for k in range(grp):
            accumulate(part, grp * g + k, grp * bset + k)

    def step(part_a, g_a, set_a, part_b, g_b, set_b):
        group_scores(part_a, g_a, set_a)
        group_accumulate(part_b, g_b, set_b)

    n_groups = [kv_refs[part][1].shape[0] * n_heads // grp for part in range(n_parts)]
    assert all(n * grp == kv_refs[part][1].shape[0] * n_heads for part, n in enumerate(n_groups))
    last = n_parts - 1
    inline = [(part, g) for part in range(last) for g in range(n_groups[part])]
    off = len(inline) % 2
    group_scores(*(inline[0] if inline else (last, 0)), 0)
    for idx, (part, g) in enumerate(inline):
        nxt = inline[idx + 1] if idx + 1 < len(inline) else (last, 0)
        step(*nxt, (idx + 1) % 2, part, g, idx % 2)

    def two_steps(j, carry):
        step(last, 2 * j + 1, 1 - off, last, 2 * j, off)
        step(last, 2 * j + 2, off, last, 2 * j + 1, 1 - off)
        return carry

    n_loop = (n_groups[last] - 1) // 2
    if n_loop > 1:
        lax.fori_loop(0, n_loop, two_steps, 0)
    else:
        for j in range(n_loop):
            two_steps(j, 0)
    for g in range(2 * n_loop, n_groups[last]):
        if g + 1 < n_groups[last]:
            step(last, g + 1, (off + g + 1) % 2, last, g, (off + g) % 2)
        else:
            group_accumulate(last, g, (off + g) % 2)

    outs = [acc_sc[hd, 0:64, :] / acc_sc[hd, 64:65, :] for hd in range(n_heads)]

    if group == "diff":
        lp = lam_ref[...]
        lam = (jnp.exp(jnp.sum(lp[0:1] * lp[1:2], axis=-1, keepdims=True))
               - jnp.exp(jnp.sum(lp[2:3] * lp[3:4], axis=-1, keepdims=True)) + lam_init)
        g = g_ref[...]
        comb = []
        for hd in range(DIFF_HEADS):
            y = outs[2 * hd] - lam * outs[2 * hd + 1]
            ms = jnp.sum(y * y, axis=0, keepdims=True) * (1.0 / DIFF_V)
            comb.append(y * lax.rsqrt(ms + EPS) * g * (1.0 - lam_init))
        outs = comb
    o_ref[...] = jnp.concatenate(outs, axis=0).T.astype(o_ref.dtype)


def _attention(group, q_list, kv_list, n_bat, seq, lam_init=0.0, extras=()):
    tq = min(Q_TILE[group], seq)
    n_q = seq // tq
    in_specs, inputs, chunk_counts = [], [], []
    for q in q_list:
        inputs.append(q)
        in_specs.append(pl.BlockSpec((tq, q.shape[1]), lambda b, i: (b * n_q + i, 0)))
    for k, vt, kspec, vspec, chunk in kv_list:
        inputs += [k, vt]
        in_specs += [kspec, vspec]
        chunk_counts.append(chunk)
    for e in extras:
        inputs.append(e)
        in_specs.append(pl.BlockSpec(e.shape, lambda b, i, _n=e.ndim: (0,) * _n))
    n_heads = {"diff": 2 * DIFF_HEADS, "mla": MLA_HEADS, "gqa": GQA_HEADS}[group]
    k_width = MLA_HEAD_W if group == "mla" else q_list[0].shape[1]
    max_chunk = max(chunk_counts)
    scratch = [pltpu.VMEM((n_heads, 1, tq), F32),
               pltpu.VMEM((n_heads, 64 + ONES_ROWS, tq), F32),
               pltpu.VMEM((len(q_list), n_heads, tq, k_width), BF16)]
    scratch += [pltpu.VMEM((max_chunk, tq), F32)] * PARK_BUFFERS
    scratch += [pltpu.VMEM((1, tq), F32)] * PARK_BUFFERS
    return pl.pallas_call(
        functools.partial(_attn_kernel, group, len(q_list), tuple(chunk_counts), lam_init),
        grid=(n_bat, n_q),
        in_specs=in_specs,
        out_specs=pl.BlockSpec((tq, GROUP_W), lambda b, i: (b * n_q + i, 0)),
        out_shape=jax.ShapeDtypeStruct((n_bat * seq, GROUP_W), BF16),
        scratch_shapes=scratch,
        compiler_params=_params(2),
        name="attn_" + group,
    )(*inputs)


def _own_kv(k, vt, seq):
    if k.ndim == 3:
        kspec = pl.BlockSpec((k.shape[0], seq, k.shape[2]), lambda b, i: (0, b, 0))
    else:
        kspec = pl.BlockSpec((seq, k.shape[1]), lambda b, i: (b, 0))
    vspec = pl.BlockSpec((None,) + vt.shape[1:], lambda b, i: (b, 0, 0, 0))
    return (k, vt, kspec, vspec, vt.shape[3])


def _cache_kv(k, vt, layer):
    past = vt.shape[3]
    vt4 = vt.reshape(vt.shape[0], vt.shape[1], 1, vt.shape[2], past)
    kspec = pl.BlockSpec((None, None) + k.shape[2:], lambda b, i, _n=k.ndim - 2: (layer, b) + (0,) * _n)
    vspec = pl.BlockSpec((None, None, 1, vt.shape[2], past), lambda b, i: (layer, b, 0, 0, 0))
    return (k, vt4, kspec, vspec, past)


def _fourier_kernel(n_bat, cos_ref, sin_ref, ucs_ref, o_ref):
    cs = cos_ref[...]
    sn = sin_ref[...]
    for b in range(n_bat):
        u = ucs_ref[b]
        y = _dot(cs, u[:, :GROUP_W]) + _dot(sn, u[:, GROUP_W:])
        o_ref[b] = y.astype(o_ref.dtype)


def _fourier(ucs, n_bat, seq, cos_tab, sin_tab):
    tm = min(FOURIER_ROWS, seq)
    u3 = ucs.reshape(n_bat, seq, 2 * GROUP_W)
    out = pl.pallas_call(
        functools.partial(_fourier_kernel, n_bat),
        grid=(seq // tm,),
        in_specs=[pl.BlockSpec((tm, seq), lambda i: (i, 0)),
                  pl.BlockSpec((tm, seq), lambda i: (i, 0)),
                  pl.BlockSpec((n_bat, seq, 2 * GROUP_W), lambda i: (0, 0, 0))],
        out_specs=pl.BlockSpec((n_bat, tm, GROUP_W), lambda i: (0, i, 0)),
        out_shape=jax.ShapeDtypeStruct((n_bat, seq, GROUP_W), BF16),
        compiler_params=_params(1),
        name="fourier",
    )(cos_tab, sin_tab, u3)
    return out.reshape(n_bat * seq, GROUP_W)


def _post_kernel(final, x_ref, od_ref, yf_ref, om_ref, og_ref, mod_ref, wout_ref, gmlp_ref,
                 w1_ref, w2_ref, gfin_ref, o_ref):
    mod = mod_ref[...]
    gate1 = mod[:, 2 * D_MODEL:3 * D_MODEL]
    shift2 = mod[:, 3 * D_MODEL:4 * D_MODEL]
    scale2 = mod[:, 4 * D_MODEL:5 * D_MODEL]
    gate2 = mod[:, 5 * D_MODEL:6 * D_MODEL]
    mix = (_dot(od_ref[...], wout_ref[0]) + _dot(yf_ref[...], wout_ref[1])
           + _dot(om_ref[...], wout_ref[2]) + _dot(og_ref[...], wout_ref[3]))
    x1 = x_ref[...] + gate1 * mix
    h = (_rms_full(x1, gmlp_ref[...], D_MODEL) * (1.0 + scale2) + shift2).astype(BF16)
    acc = jnp.zeros(x1.shape, F32)
    for c in range(D_FF // MLP_CHUNK):
        sl = slice(c * MLP_CHUNK, (c + 1) * MLP_CHUNK)
        a = jnp.maximum(_dot(h, w1_ref[:, sl]), 0.0)
        acc = acc + _dot((a * a).astype(BF16), w2_ref[sl, :])
    x2 = x1 + gate2 * acc
    if final:
        x2 = _rms_full(x2, gfin_ref[...], D_MODEL)
    o_ref[...] = x2


def _post(x2d, o_diff, y_f, o_mla, o_gqa, mods3, mod_row_fn, lw, g_final, final, name):
    n_tok = x2d.shape[0]
    tb = TOKEN_BLOCK
    tok = lambda w: pl.BlockSpec((tb, w), lambda i: (i, 0))
    const = lambda a: pl.BlockSpec(a.shape, lambda i, _n=a.ndim: (0,) * _n)
    return pl.pallas_call(
        functools.partial(_post_kernel, final),
        grid=(n_tok // tb,),
        in_specs=[tok(D_MODEL), tok(GROUP_W), tok(GROUP_W), tok(GROUP_W), tok(GROUP_W),
                  pl.BlockSpec((None, 1, N_MOD * D_MODEL), lambda i: (mod_row_fn(i), 0, 0)),
                  const(lw["w_out"]), const(lw["gmlp"]), const(lw["w1"]), const(lw["w2"]), const(g_final)],
        out_specs=tok(D_MODEL),
        out_shape=jax.ShapeDtypeStruct((n_tok, D_MODEL), F32),
        compiler_params=_params(1),
        name=name,
    )(x2d, o_diff, y_f, o_mla, o_gqa, mods3, lw["w_out"], lw["gmlp"], lw["w1"], lw["w2"], g_final)


def _rope_tables(n_lat, head_dim, width, lane0=0):
    a = head_dim // 2
    inv = ROPE_THETA ** (-jnp.arange(0, a, 2, dtype=F32) / a)
    n_rows = n_lat // GRID_W
    lane = jnp.arange(width, dtype=jnp.int32) - lane0
    in_pat = (lane >= 0) & ((lane < head_dim) | (lane0 == 0))
    li = jnp.where(in_pat, lane % head_dim, 0)
    part = li // a
    ii = li % a
    first = (ii // (a // 2)) == 0
    f = ii % (a // 2)
    g = jnp.arange(max(n_rows, GRID_W), dtype=jnp.int32).astype(F32)
    ang = g[:, None] * inv[f][None, :]
    cos_g = jnp.where(in_pat[None, :], jnp.cos(ang), 1.0)
    ssin_g = jnp.where(in_pat[None, :], jnp.where(first[None, :], -jnp.sin(ang), jnp.sin(ang)), 0.0)

    def spread(tab):
        by_row = jnp.repeat(tab[:n_rows], GRID_W, axis=0)
        by_col = jnp.tile(tab[:GRID_W], (n_rows, 1))
        return jnp.where(part[None, :] == 0, by_row, by_col)

    return spread(cos_g), spread(ssin_g)


def _dft_tables(n):
    s = n ** -0.5
    k = jnp.arange(n, dtype=jnp.int32)

    def cos_sin(rows, scale_rows):
        r = jnp.arange(rows, dtype=jnp.int32) * scale_rows
        ang = ((r[:, None] * k[None, :]) % n).astype(F32) * (2.0 * math.pi / n)
        return jnp.cos(ang), jnp.sin(ang)

    if n <= 4 * GRID_W:
        c, sn = cos_sin(n, 1)
        return (c * s).astype(BF16), (sn * s).astype(BF16)
    n1 = GRID_W
    ca, sa = cos_sin(n // n1, n1)
    cb, sb = cos_sin(n1, 1)
    cos = ca[:, None, :] * cb[None, :, :] - sa[:, None, :] * sb[None, :, :]
    sin = sa[:, None, :] * cb[None, :, :] + ca[:, None, :] * sb[None, :, :]
    return (cos * s).astype(BF16).reshape(n, n), (sin * s).astype(BF16).reshape(n, n)


def _constants():
    eye = lambda n: jnp.eye(n, dtype=F32)
    ones_q = jnp.kron(eye(GQA_HEADS), jnp.ones((GQA_DIM, GQA_DIM), F32)).astype(BF16)
    ones_k = jnp.kron(eye(GQA_KV_HEADS), jnp.ones((GQA_DIM, GQA_DIM), F32)).astype(BF16)
    dup = jnp.kron(eye(GQA_KV_HEADS), jnp.concatenate([eye(GQA_DIM), eye(GQA_DIM)], axis=1)).astype(BF16)
    one_head = jnp.concatenate([jnp.zeros((MLA_ROPE, KR_LANE), F32), eye(MLA_ROPE),
                                jnp.zeros((MLA_ROPE, MLA_HEAD_W - KR_LANE - MLA_ROPE), F32)], axis=1)
    place_kr = jnp.tile(one_head, (1, MLA_HEADS)).astype(BF16)
    j = jnp.arange(FNET_CH, dtype=jnp.int32)
    ang = ((j[:, None] * j[None, :]) % FNET_CH).astype(F32) * (2.0 * math.pi / FNET_CH)
    cc = jnp.kron(eye(FNET_GROUPS), jnp.cos(ang) * FNET_CH ** -0.5)
    sc = jnp.kron(eye(FNET_GROUPS), jnp.sin(ang) * FNET_CH ** -0.5)
    csc = jnp.concatenate([cc, -sc], axis=1).astype(BF16)
    return dict(ones_q=ones_q, ones_k=ones_k, dup=dup, place_kr=place_kr, csc=csc)


def _layer_weights(l, norm_mix_g, norm_mlp_g, w_in, mla_q_norm_g, mla_w_uq, mla_kv_norm_g, mla_w_ukv,
                   gqa_q_norm_g, gqa_k_norm_g, w_out, mlp_w1, mlp_w2):
    wi = w_in[l]
    zc = lambda n: jnp.zeros((D_MODEL, n), F32)
    off = [0, 256, 512, 768, 1024, 1216, 1344, 1376, 1632, 1760, 1888]
    seg = [wi[:, off[k]:off[k + 1]] for k in range(10)]
    w_in_p = jnp.concatenate([
        seg[0], seg[1], seg[2], seg[3],
        seg[4], zc(256 - MLA_Q_RANK),
        seg[5],
        zc(KR_LANE), seg[6], zc(MLA_HEAD_W - KR_LANE - MLA_ROPE),
        seg[7], seg[8], seg[9]], axis=1).astype(BF16)
    wuq = mla_w_uq[l].reshape(MLA_Q_RANK, MLA_HEADS, MLA_NOPE + MLA_ROPE)
    wuq = jnp.pad(wuq, ((0, 256 - MLA_Q_RANK), (0, 0), (0, MLA_HEAD_W - MLA_NOPE - MLA_ROPE)))
    w_uq_p = wuq.reshape(256, MLA_HEADS * MLA_HEAD_W).astype(BF16)
    wukv = mla_w_ukv[l].reshape(MLA_KV_RANK, MLA_HEADS, MLA_NOPE + MLA_V)
    wk = jnp.pad(wukv[:, :, :MLA_NOPE], ((0, 0), (0, 0), (0, MLA_HEAD_W - MLA_NOPE)))
    wv = wukv[:, :, MLA_NOPE:]
    w_ukv_p = jnp.concatenate([wk.reshape(MLA_KV_RANK, -1), wv.reshape(MLA_KV_RANK, -1)], axis=1).astype(BF16)
    return dict(
        gmix=norm_mix_g[l][None, :], gmlp=norm_mlp_g[l][None, :], w_in=w_in_p,
        gq=jnp.pad(mla_q_norm_g[l], (0, 256 - MLA_Q_RANK))[None, :], w_uq=w_uq_p,
        gkv=mla_kv_norm_g[l][None, :], w_ukv=w_ukv_p,
        gqn=jnp.tile(gqa_q_norm_g[l], GQA_HEADS)[None, :], gkn=jnp.tile(gqa_k_norm_g[l], GQA_KV_HEADS)[None, :],
        w_out=w_out[l].reshape(4, GROUP_W, D_MODEL).astype(BF16),
        w1=mlp_w1[l].astype(BF16), w2=mlp_w2[l].astype(BF16))


def kernel(x_prompt, x_sample, cache_diff_k, cache_diff_v, cache_mla_ckv, cache_mla_krope, cache_gqa_k, cache_gqa_v, c, c_ctx, norm_mix_g, norm_mlp_g, ada_w, ada_b, w_in, diff_lambda, diff_subln_g, mla_q_norm_g, mla_w_uq, mla_kv_norm_g, mla_w_ukv, gqa_q_norm_g, gqa_k_norm_g, w_out, mlp_w1, mlp_w2, final_norm_g):
    n_ctx_b, seq, _ = x_prompt.shape
    n_lat_b, n_lat, _ = x_sample.shape
    depth = w_in.shape[0]
    assert seq % 128 == 0 and TOKEN_BLOCK % seq == 0 and (n_ctx_b * seq) % TOKEN_BLOCK == 0
    assert n_lat % TOKEN_BLOCK == 0 and n_lat % KEY_CHUNK == 0 and n_lat % GRID_W == 0

    consts = _constants()
    rows = 1 + n_lat_b
    rows_pad = -(-rows // 8) * 8
    cs = jnp.concatenate([c_ctx[None, :], c, jnp.zeros((rows_pad - rows, D_MODEL), F32)], axis=0)
    mods = _adaln(cs, ada_w, ada_b).reshape(depth * rows_pad, 1, N_MOD * D_MODEL)

    rope_tabs = (_rope_tables(n_lat, DIFF_QK, GROUP_W) + _rope_tables(n_lat, MLA_ROPE, MLA_HEAD_W, KR_LANE)
                 + _rope_tables(n_lat, GQA_DIM, GROUP_W))
    dft_ctx = _dft_tables(seq)
    dft_lat = _dft_tables(n_lat)

    layers = [_layer_weights(l, norm_mix_g, norm_mlp_g, w_in, mla_q_norm_g, mla_w_uq, mla_kv_norm_g,
                             mla_w_ukv, gqa_q_norm_g, gqa_k_norm_g, w_out, mlp_w1, mlp_w2)
              for l in range(depth)]
    cache = _cache_prep(cache_diff_k, cache_diff_v, cache_mla_ckv, cache_mla_krope, cache_gqa_k, cache_gqa_v,
                        jnp.stack([lw["w_ukv"] for lw in layers]), consts)

    g_final = final_norm_g[None, :]
    xp = x_prompt.reshape(n_ctx_b * seq, D_MODEL)
    xs = x_sample.reshape(n_lat_b * n_lat, D_MODEL)
    lat_blocks = n_lat // TOKEN_BLOCK
    states = [[] for _ in range(6)]
    for l in range(depth):
        lw = layers[l]
        lam_init = 0.8 - 0.6 * math.exp(-0.3 * l)
        ctx_row = lambda i, _l=l: _l * rows_pad
        lat_row = lambda i, _l=l: _l * rows_pad + 1 + i // lat_blocks
        lam_p = diff_lambda[l].astype(F32)
        final = l == depth - 1

        t = _pre(xp, mods, ctx_row, lw, consts, seq, latent=False)
        g_t = jnp.broadcast_to(diff_subln_g[l][:, None], (DIFF_V, min(Q_TILE["diff"], seq)))
        o_a = _attention("diff", [t["qd"]], [_own_kv(t["kd"], t["vdt"], seq)], n_ctx_b, seq, lam_init,
                         extras=(lam_p, g_t))
        o_c = _attention("mla", [t["qm"]], [_own_kv(t["km"], t["vmt"], seq)], n_ctx_b, seq)
        o_d = _attention("gqa", [t["qg"]], [_own_kv(t["kg"], t["vgt"], seq)], n_ctx_b, seq)
        y_f = _fourier(t["ucs"], n_ctx_b, seq, *dft_ctx)
        xp = _post(xp, o_a, y_f, o_c, o_d, mods, ctx_row, lw, g_final, final, "post_context")
        for k, nm in enumerate(["s_diff_k", "s_diff_v", "s_ckv", "s_kr", "s_gqa_k", "s_gqa_v"]):
            states[k].append(t[nm])

        t = _pre(xs, mods, lat_row, lw, consts, n_lat, latent=True, rope_tabs=rope_tabs)
        g_t = jnp.broadcast_to(diff_subln_g[l][:, None], (DIFF_V, min(Q_TILE["diff"], n_lat)))
        o_a = _attention("diff", [t["qd"], t["qd_rot"]],
                         [_cache_kv(cache["kd"], cache["vdt"], l), _own_kv(t["kd"], t["vdt"], n_lat)],
                         n_lat_b, n_lat, lam_init, extras=(lam_p, g_t))
        o_c = _attention("mla", [t["qm"], t["qm_rot"]],
                         [_cache_kv(cache["km"], cache["vmt"], l), _own_kv(t["km"], t["vmt"], n_lat)],
                         n_lat_b, n_lat)
        o_d = _attention("gqa", [t["qg"], t["qg_rot"]],
                         [_cache_kv(cache["kg"], cache["vgt"], l), _own_kv(t["kg"], t["vgt"], n_lat)],
                         n_lat_b, n_lat)
        y_f = _fourier(t["ucs"], n_lat_b, n_lat, *dft_lat)
        xs = _post(xs, o_a, y_f, o_c, o_d, mods, lat_row, lw, g_final, final, "post_latent")

    y_prompt = xp.reshape(n_ctx_b, seq, D_MODEL)
    y_sample = xs.reshape(n_lat_b, n_lat, D_MODEL)
    st = [jnp.stack([a.reshape(n_ctx_b, seq, -1) for a in s], axis=1) for s in states]
    new_diff_k = st[0].reshape(n_ctx_b, depth, seq, DIFF_HEADS, 2 * DIFF_QK)
    new_diff_v = st[1].reshape(n_ctx_b, depth, seq, DIFF_HEADS, DIFF_V)
    new_gqa_k = st[4].reshape(n_ctx_b, depth, seq, GQA_KV_HEADS, GQA_DIM)
    new_gqa_v = st[5].reshape(n_ctx_b, depth, seq, GQA_KV_HEADS, GQA_DIM)
    return (y_prompt, y_sample, new_diff_k, new_diff_v, st[2], st[3], new_gqa_k, new_gqa_v)
```
